```python
import math
import jax
import jax.numpy as jnp
from jax import lax
import numpy as np

D_MODEL = 4096
BATCH = 4
SEQ = 2048
DEPTH = 2
DEC_BATCH = 8
DEC_SEQ = 8
PAST_LEN = 16384
PAGE_SIZE = 128

N_MIXERS = 2
N_SSM_LAYERS = (DEPTH + 1) // 2
N_ATTN_LAYERS = DEPTH // 2

MEM_TOKENS = 256
MEM_HEADS = 4
MEM_HEAD_DIM = D_MODEL // 16
MEM_WIDTH = MEM_HEADS * MEM_HEAD_DIM
MIX_WIDTH = D_MODEL - MEM_WIDTH

SSM_GROUP = 16
SSM_GROUPS = MIX_WIDTH // SSM_GROUP
SSM_STATE = 64
DT_MIN = 1e-3
DT_MAX = 1e-1

DIFF_HEAD_DIM = 128
DIFF_HEADS = MIX_WIDTH // (2 * DIFF_HEAD_DIM)
DIFF_V_DIM = 2 * DIFF_HEAD_DIM

IN_SSM = MIX_WIDTH + MEM_WIDTH
IN_ATTN = 3 * MIX_WIDTH + MEM_WIDTH

D_FF = -(-8 * D_MODEL // (3 * 256)) * 256
ALPHA = (2 * DEPTH) ** 0.25
BETA = (8 * DEPTH) ** -0.25
Q_BLOCK = 128
LN_EPS = 1e-5

kernel_name = 'hybrid_s5_diffattn_memxattn_decoder_step'


def layer_norm(x, g, b):
    xf = x.astype(jnp.float32)
    mu = jnp.mean(xf, -1, keepdims=True)
    var = jnp.mean(jnp.square(xf - mu), -1, keepdims=True)
    return ((xf - mu) * lax.rsqrt(var + LN_EPS) * g.astype(jnp.float32) + b.astype(jnp.float32)).astype(x.dtype)


def swiglu(x, w_in, w_out):
    g, u = jnp.split(x @ w_in, 2, axis=-1)
    return (jax.nn.silu(g) * u) @ w_out


def _cmul(ar, ai, br, bi):
    return ar * br - ai * bi, ar * bi + ai * br


def _s5_combine(left, right):
    alr, ali, blr, bli = left
    arr, ari, brr, bri = right
    ar, ai = _cmul(arr, ari, alr, ali)
    tr, ti = _cmul(arr, ari, blr, bli)
    return ar, ai, tr + brr, ti + bri


def s5_mixer(u, s0, lam_re, lam_im, log_dt, b_re, b_im, c_re, c_im, d_skip, w_glu, b_glu):
    bsz, slen = u.shape[:2]
    f32 = jnp.float32
    ug = u.astype(f32).reshape(bsz, slen, SSM_GROUPS, SSM_GROUP)
    dt = jnp.exp(log_dt.astype(f32))[:, None]
    lr = lam_re.astype(f32)
    li = lam_im.astype(f32)
    mag = jnp.exp(lr * dt)
    ab_re = mag * jnp.cos(li * dt)
    ab_im = mag * jnp.sin(li * dt)
    den = lr * lr + li * li
    nr = ab_re - 1.0
    f_re = (nr * lr + ab_im * li) / den
    f_im = (ab_im * lr - nr * li) / den
    br = b_re.astype(f32)
    bi = b_im.astype(f32)
    bb_re = f_re[..., None] * br - f_im[..., None] * bi
    bb_im = f_re[..., None] * bi + f_im[..., None] * br
    bu_re = jnp.einsum('bsgc,gpc->bsgp', ug, bb_re)
    bu_im = jnp.einsum('bsgc,gpc->bsgp', ug, bb_im)
    if s0 is not None:
        s_re0 = s0[0].astype(f32)
        s_im0 = s0[1].astype(f32)
        bu_re = bu_re.at[:, 0].add(ab_re * s_re0 - ab_im * s_im0)
        bu_im = bu_im.at[:, 0].add(ab_re * s_im0 + ab_im * s_re0)
    a_re = jnp.broadcast_to(ab_re, (1, slen) + ab_re.shape)
    a_im = jnp.broadcast_to(ab_im, (1, slen) + ab_im.shape)
    _, _, s_re, s_im = lax.associative_scan(_s5_combine, (a_re, a_im, bu_re, bu_im), axis=1)
    y = (jnp.einsum('bsgp,gcp->bsgc', s_re, c_re.astype(f32))
         - jnp.einsum('bsgp,gcp->bsgc', s_im, c_im.astype(f32))
         + d_skip.astype(f32) * ug)
    y = jax.nn.gelu(y.reshape(bsz, slen, MIX_WIDTH)).astype(u.dtype)
    y = y * jax.nn.sigmoid(y @ w_glu + b_glu)
    return y, s_re[:, -1], s_im[:, -1]


def diff_weights(s, lam):
    p = jax.nn.softmax(s, axis=-1)
    return p[:, :, 0] - lam * p[:, :, 1]


def diff_attn_prompt(q, k, v, lam):
    bsz, slen = q.shape[:2]
    nb = slen // Q_BLOCK
    scale = DIFF_HEAD_DIM ** -0.5
    qb = q.reshape(bsz, nb, Q_BLOCK, DIFF_HEADS, 2, DIFF_HEAD_DIM).transpose(1, 0, 2, 3, 4, 5)
    kpos = jnp.arange(slen)

    def block(args):
        qi, bi = args
        s = jnp.einsum('bqhcd,bkhcd->bhcqk', qi, k, preferred_element_type=jnp.float32) * scale
        qpos = bi * Q_BLOCK + jnp.arange(Q_BLOCK)
        s = jnp.where(kpos[None, :] <= qpos[:, None], s, -jnp.inf)
        a = diff_weights(s, lam)
        return jnp.einsum('bhqk,bkhe->bqhe', a.astype(v.dtype), v)

    o = lax.map(block, (qb, jnp.arange(nb)))
    return o.transpose(1, 0, 2, 3, 4).reshape(bsz, slen, DIFF_HEADS, DIFF_V_DIM)


def diff_attn_sample(q, k_new, v_new, k_past, v_past, lam):
    scale = DIFF_HEAD_DIM ** -0.5
    tlen = q.shape[1]
    plen = k_past.shape[1]
    s_past = jnp.einsum('bthcd,bkhcd->bhctk', q, k_past, preferred_element_type=jnp.float32) * scale
    s_new = jnp.einsum('bthcd,bkhcd->bhctk', q, k_new, preferred_element_type=jnp.float32) * scale
    s_new = jnp.where(jnp.tril(jnp.ones((tlen, tlen), dtype=bool)), s_new, -jnp.inf)
    a = diff_weights(jnp.concatenate([s_past, s_new], axis=-1), lam)
    return (jnp.einsum('bhtk,bkhe->bthe', a[..., :plen].astype(v_past.dtype), v_past)
            + jnp.einsum('bhtk,bkhe->bthe', a[..., plen:].astype(v_new.dtype), v_new))


def diff_heads_out(o, subln_g, lam_init):
    of = o.astype(jnp.float32)
    of = of * lax.rsqrt(jnp.mean(of * of, -1, keepdims=True) + LN_EPS) * subln_g.astype(jnp.float32)
    return (of * (1.0 - lam_init)).astype(o.dtype).reshape(o.shape[0], o.shape[1], MIX_WIDTH)


def mem_attend(q, mk, mv):
    s = jnp.einsum('bshd,bmhd->bhsm', q, mk, preferred_element_type=jnp.float32) * (MEM_HEAD_DIM ** -0.5)
    p = jax.nn.softmax(s, axis=-1)
    o = jnp.einsum('bhsm,bmhd->bshd', p.astype(mv.dtype), mv)
    return o.reshape(q.shape[0], q.shape[1], MEM_WIDTH)


def run_trunk(x, mem_k, mem_v, ssm_state, attn_past, w):
    bsz, slen = x.shape[:2]
    new_k, new_v, new_sre, new_sim = [], [], [], []
    for i in range(DEPTH):
        li = i // N_MIXERS
        if i % N_MIXERS == 0:
            h = x @ w['w_in_ssm'][li]
            u, qm = h[..., :MIX_WIDTH], h[..., MIX_WIDTH:]
            s0 = None if ssm_state is None else (ssm_state[0][li], ssm_state[1][li])
            mix, s_re, s_im = s5_mixer(u, s0, w['ssm_lambda_re'][li], w['ssm_lambda_im'][li], w['ssm_log_dt'][li],
                                       w['ssm_b_re'][li], w['ssm_b_im'][li], w['ssm_c_re'][li], w['ssm_c_im'][li],
                                       w['ssm_d'][li], w['w_glu'][li], w['b_glu'][li])
            new_sre.append(s_re)
            new_sim.append(s_im)
        else:
            h = x @ w['w_in_attn'][li]
            q = h[..., :MIX_WIDTH].reshape(bsz, slen, DIFF_HEADS, 2, DIFF_HEAD_DIM)
            k = h[..., MIX_WIDTH:2 * MIX_WIDTH].reshape(bsz, slen, DIFF_HEADS, 2, DIFF_HEAD_DIM)
            v = h[..., 2 * MIX_WIDTH:3 * MIX_WIDTH].reshape(bsz, slen, DIFF_HEADS, DIFF_V_DIM)
            qm = h[..., 3 * MIX_WIDTH:]
            lam_init = 0.8 - 0.6 * math.exp(-0.3 * i)
            lq1 = w['diff_lambda_q1'][li].astype(jnp.float32)
            lk1 = w['diff_lambda_k1'][li].astype(jnp.float32)
            lq2 = w['diff_lambda_q2'][li].astype(jnp.float32)
            lk2 = w['diff_lambda_k2'][li].astype(jnp.float32)
            lam = jnp.exp(jnp.sum(lq1 * lk1)) - jnp.exp(jnp.sum(lq2 * lk2)) + lam_init
            if attn_past is None:
                o = diff_attn_prompt(q, k, v, lam)
            else:
                k_pool, v_pool, page_table = attn_past
                k_past = k_pool[li, page_table].reshape(bsz, -1, DIFF_HEADS, 2, DIFF_HEAD_DIM)
                v_past = v_pool[li, page_table].reshape(bsz, -1, DIFF_HEADS, DIFF_V_DIM)
                o = diff_attn_sample(q, k, v, k_past, v_past, lam)
            mix = diff_heads_out(o, w['diff_subln_g'][li], lam_init)
            new_k.append(k)
            new_v.append(v)
        m = mem_attend(qm.reshape(bsz, slen, MEM_HEADS, MEM_HEAD_DIM), mem_k[i], mem_v[i])
        o = jnp.concatenate([mix, m.astype(mix.dtype)], axis=-1) @ w['w_o'][i]
        x = layer_norm(ALPHA * x + o, w['ln1_g'][i], w['ln1_b'][i])
        x = layer_norm(ALPHA * x + swiglu(x, w['w_ffn_in'][i], w['w_ffn_out'][i]), w['ln2_g'][i], w['ln2_b'][i])
    return x, jnp.stack(new_k), jnp.stack(new_v), jnp.stack(new_sre), jnp.stack(new_sim)


def setup_inputs(seed: int = 0) -> dict:
    key = jax.random.key(seed)
    ks = jax.random.split(key, 40)
    n_pages = PAST_LEN // PAGE_SIZE
    n_used = DEC_BATCH * n_pages
    n_pool = n_used + max(1, n_used // 4)

    def nrm(i, shape, scale=1.0):
        return jax.random.normal(ks[i], shape, jnp.float32) * scale

    page_table = jax.random.permutation(ks[0], n_pool)[:n_used].reshape(DEC_BATCH, n_pages).astype(jnp.int32)
    ssm_gp = (N_SSM_LAYERS, SSM_GROUPS, SSM_STATE)
    return {
        'x_prompt': nrm(1, (BATCH, SEQ, D_MODEL)),
        'x_sample': nrm(2, (DEC_BATCH, DEC_SEQ, D_MODEL)),
        'cache_attn_k': nrm(3, (N_ATTN_LAYERS, n_pool, PAGE_SIZE, DIFF_HEADS, 2, DIFF_HEAD_DIM)),
        'cache_attn_v': nrm(4, (N_ATTN_LAYERS, n_pool, PAGE_SIZE, DIFF_HEADS, DIFF_V_DIM)),
        'state_ssm_re': nrm(5, (N_SSM_LAYERS, DEC_BATCH, SSM_GROUPS, SSM_STATE), 0.5),
        'state_ssm_im': nrm(6, (N_SSM_LAYERS, DEC_BATCH, SSM_GROUPS, SSM_STATE), 0.5),
        'cache_mem_k': nrm(7, (DEPTH, DEC_BATCH, MEM_TOKENS, MEM_HEADS, MEM_HEAD_DIM)),
        'cache_mem_v': nrm(8, (DEPTH, DEC_BATCH, MEM_TOKENS, MEM_HEADS, MEM_HEAD_DIM)),
        'page_table': page_table,
        'mem_prompt': nrm(9, (BATCH, MEM_TOKENS, D_MODEL)),
        'w_in_ssm': nrm(10, (N_SSM_LAYERS, D_MODEL, IN_SSM), D_MODEL ** -0.5),
        'ssm_lambda_re': -0.5 + nrm(11, ssm_gp, 0.01),
        'ssm_lambda_im': jnp.pi * jnp.arange(SSM_STATE, dtype=jnp.float32) + nrm(12, ssm_gp, 0.01),
        'ssm_log_dt': jax.random.uniform(ks[13], (N_SSM_LAYERS, SSM_GROUPS), jnp.float32, math.log(DT_MIN), math.log(DT_MAX)),
        'ssm_b_re': nrm(14, (N_SSM_LAYERS, SSM_GROUPS, SSM_STATE, SSM_GROUP), (2 * SSM_GROUP) ** -0.5),
        'ssm_b_im': nrm(15, (N_SSM_LAYERS, SSM_GROUPS, SSM_STATE, SSM_GROUP), (2 * SSM_GROUP) ** -0.5),
        'ssm_c_re': nrm(16, (N_SSM_LAYERS, SSM_GROUPS, SSM_GROUP, SSM_STATE), SSM_STATE ** -0.5),
        'ssm_c_im': nrm(17, (N_SSM_LAYERS, SSM_GROUPS, SSM_GROUP, SSM_STATE), SSM_STATE ** -0.5),
        'ssm_d': nrm(18, (N_SSM_LAYERS, SSM_GROUPS, SSM_GROUP)),
        'w_glu': nrm(19, (N_SSM_LAYERS, MIX_WIDTH, MIX_WIDTH), MIX_WIDTH ** -0.5),
        'b_glu': nrm(20, (N_SSM_LAYERS, MIX_WIDTH), 0.02),
        'w_in_attn': nrm(21, (N_ATTN_LAYERS, D_MODEL, IN_ATTN), D_MODEL ** -0.5),
        'diff_lambda_q1': nrm(22, (N_ATTN_LAYERS, DIFF_HEAD_DIM), 0.1),
        'diff_lambda_k1': nrm(23, (N_ATTN_LAYERS, DIFF_HEAD_DIM), 0.1),
        'diff_lambda_q2': nrm(24, (N_ATTN_LAYERS, DIFF_HEAD_DIM), 0.1),
        'diff_lambda_k2': nrm(25, (N_ATTN_LAYERS, DIFF_HEAD_DIM), 0.1),
        'diff_subln_g': 1.0 + nrm(26, (N_ATTN_LAYERS, DIFF_V_DIM), 0.02),
        'w_mem_kv': nrm(27, (DEPTH, D_MODEL, 2 * MEM_WIDTH), D_MODEL ** -0.5),
        'w_o': nrm(28, (DEPTH, D_MODEL, D_MODEL), BETA * D_MODEL ** -0.5),
        'ln1_g': 1.0 + nrm(29, (DEPTH, D_MODEL), 0.02),
        'ln1_b': nrm(30, (DEPTH, D_MODEL), 0.02),
        'w_ffn_in': nrm(31, (DEPTH, D_MODEL, 2 * D_FF), D_MODEL ** -0.5),
        'w_ffn_out': nrm(32, (DEPTH, D_FF, D_MODEL), BETA * D_FF ** -0.5),
        'ln2_g': 1.0 + nrm(33, (DEPTH, D_MODEL), 0.02),
        'ln2_b': nrm(34, (DEPTH, D_MODEL), 0.02),
    }


def reference(x_prompt, x_sample, cache_attn_k, cache_attn_v, state_ssm_re, state_ssm_im, cache_mem_k, cache_mem_v,
              page_table, mem_prompt, w_in_ssm, ssm_lambda_re, ssm_lambda_im, ssm_log_dt, ssm_b_re, ssm_b_im,
              ssm_c_re, ssm_c_im, ssm_d, w_glu, b_glu, w_in_attn, diff_lambda_q1, diff_lambda_k1, diff_lambda_q2,
              diff_lambda_k2, diff_subln_g, w_mem_kv, w_o, ln1_g, ln1_b, w_ffn_in, w_ffn_out, ln2_g, ln2_b):
    w = dict(w_in_ssm=w_in_ssm, ssm_lambda_re=ssm_lambda_re, ssm_lambda_im=ssm_lambda_im, ssm_log_dt=ssm_log_dt,
             ssm_b_re=ssm_b_re, ssm_b_im=ssm_b_im, ssm_c_re=ssm_c_re, ssm_c_im=ssm_c_im, ssm_d=ssm_d,
             w_glu=w_glu, b_glu=b_glu, w_in_attn=w_in_attn, diff_lambda_q1=diff_lambda_q1,
             diff_lambda_k1=diff_lambda_k1, diff_lambda_q2=diff_lambda_q2, diff_lambda_k2=diff_lambda_k2,
             diff_subln_g=diff_subln_g, w_o=w_o, ln1_g=ln1_g, ln1_b=ln1_b, w_ffn_in=w_ffn_in,
             w_ffn_out=w_ffn_out, ln2_g=ln2_g, ln2_b=ln2_b)
    bsz = mem_prompt.shape[0]
    mem_kv = jnp.einsum('bmd,ldf->lbmf', mem_prompt, w_mem_kv)
    mem_k_prompt = mem_kv[..., :MEM_WIDTH].reshape(DEPTH, bsz, MEM_TOKENS, MEM_HEADS, MEM_HEAD_DIM)
    mem_v_prompt = mem_kv[..., MEM_WIDTH:].reshape(DEPTH, bsz, MEM_TOKENS, MEM_HEADS, MEM_HEAD_DIM)
    y_prompt, k_prompt, v_prompt, ssm_re_prompt, ssm_im_prompt = run_trunk(
        x_prompt, mem_k_prompt, mem_v_prompt, None, None, w)
    y_sample, k_sample, v_sample, ssm_re_sample, ssm_im_sample = run_trunk(
        x_sample, cache_mem_k, cache_mem_v, (state_ssm_re, state_ssm_im),
        (cache_attn_k, cache_attn_v, page_table), w)
    return (y_prompt, y_sample, k_prompt, v_prompt, k_sample, v_sample,
            ssm_re_prompt, ssm_im_prompt, ssm_re_sample, ssm_im_sample, mem_k_prompt, mem_v_prompt)
```

```python
import functools
import math

import jax
import jax.numpy as jnp
from jax import lax
from jax.experimental import pallas as pl
from jax.experimental.pallas import tpu as pltpu

F32 = jnp.float32
BF16 = jnp.bfloat16

D_MODEL = 4096
DEPTH = 2
MEM_TOKENS = 256
MEM_HEADS = 4
MEM_HEAD_DIM = 256
MEM_WIDTH = MEM_HEADS * MEM_HEAD_DIM
MIX_WIDTH = D_MODEL - MEM_WIDTH
SSM_GROUP = 16
SSM_GROUPS = MIX_WIDTH // SSM_GROUP
SSM_STATE = 64
DIFF_HEAD_DIM = 128
DIFF_HEADS = MIX_WIDTH // (2 * DIFF_HEAD_DIM)
DIFF_V_DIM = 2 * DIFF_HEAD_DIM
D_FF = 11008
ALPHA = (2 * DEPTH) ** 0.25
LN_EPS = 1e-5
PAGE_SIZE = 128

VMEM_LIMIT_BYTES = 58 * 1024 * 1024
LANES = 128
SUBLANES = 8

S5_GROUPS_PER_BLOCK = 16
S5_CH = S5_GROUPS_PER_BLOCK * SSM_GROUP
S5_HALF = S5_GROUPS_PER_BLOCK * SSM_STATE
S5_BLOCKS = SSM_GROUPS // S5_GROUPS_PER_BLOCK
S5_SEG = 64
S5_PITCH = S5_SEG + SUBLANES
LN_ROWS = 128


def _params(*sem):
    return pltpu.CompilerParams(dimension_semantics=sem, vmem_limit_bytes=VMEM_LIMIT_BYTES)


def _dot(a, b):
    return jnp.dot(a, b, preferred_element_type=F32)


def _dot_nt(a, b):
    return lax.dot_general(a, b, (((1,), (1,)), ((), ())), preferred_element_type=F32)


def _mm_kernel(x_ref, w_ref, o_ref):
    @pl.when(pl.program_id(2) == 0)
    def _():
        o_ref[...] = jnp.zeros_like(o_ref)

    o_ref[...] += _dot(x_ref[...].astype(BF16), w_ref[...].astype(BF16))


def _matmul(x, w, layer, *, tm, tn, tk):
    m, k = x.shape
    n = w.shape[2]
    return pl.pallas_call(
        _mm_kernel,
        grid=(m // tm, n // tn, k // tk),
        in_specs=[
            pl.BlockSpec((tm, tk), lambda i, j, kk: (i, kk)),
            pl.BlockSpec((None, tk, tn), lambda i, j, kk: (layer, kk, j)),
        ],
        out_specs=pl.BlockSpec((tm, tn), lambda i, j, kk: (i, j)),
        out_shape=jax.ShapeDtypeStruct((m, n), F32),
        compiler_params=_params("parallel", "parallel", "arbitrary"),
        name="matmul",
    )(x, w)


def _glu_kernel(yk_ref, w_ref, b_ref, yn_ref, o_ref, acc_ref):
    kk = pl.program_id(2)

    @pl.when(kk == 0)
    def _():
        acc_ref[...] = jnp.zeros_like(acc_ref)

    acc_ref[...] += _dot(yk_ref[...].astype(BF16), w_ref[...].astype(BF16))

    @pl.when(kk == pl.num_programs(2) - 1)
    def _():
        gate = jax.nn.sigmoid(acc_ref[...] + b_ref[...])
        o_ref[...] = (yn_ref[...] * gate).astype(o_ref.dtype)


def _act_dtype(rows_per_block):
    return BF16 if rows_per_block % (2 * SUBLANES) == 0 else F32


def _glu(y, w, b, layer, out_dtype, *, tm, tn, tk):
    m, k = y.shape
    n = w.shape[2]
    return pl.pallas_call(
        _glu_kernel,
        grid=(m // tm, n // tn, k // tk),
        in_specs=[
            pl.BlockSpec((tm, tk), lambda i, j, kk: (i, kk)),
            pl.BlockSpec((None, tk, tn), lambda i, j, kk: (layer, kk, j)),
            pl.BlockSpec((None, 1, tn), lambda i, j, kk: (layer, 0, j)),
            pl.BlockSpec((tm, tn), lambda i, j, kk: (i, j)),
        ],
        out_specs=pl.BlockSpec((tm, tn), lambda i, j, kk: (i, j)),
        out_shape=jax.ShapeDtypeStruct((m, n), out_dtype),
        scratch_shapes=[pltpu.VMEM((tm, tn), F32)],
        compiler_params=_params("parallel", "parallel", "arbitrary"),
        name="glu",
    )(y, w, b.reshape(b.shape[0], 1, b.shape[1]), y)


def _mm_ln_kernel(a_ref, w_ref, x_ref, g_ref, b_ref, of_ref, ob_ref):
    kk = pl.program_id(1)

    @pl.when(kk == 0)
    def _():
        of_ref[...] = jnp.zeros_like(of_ref)

    of_ref[...] += _dot(a_ref[...].astype(BF16), w_ref[...].astype(BF16))

    @pl.when(kk == pl.num_programs(1) - 1)
    def _():
        tm = of_ref.shape[0]
        rc = min(tm, LN_ROWS)

        def rows(c, _):
            r = pl.ds(pl.multiple_of(c * rc, rc), rc)
            y = ALPHA * x_ref[r, :] + of_ref[r, :]
            mu = jnp.mean(y, axis=-1, keepdims=True)
            d = y - mu
            var = jnp.mean(d * d, axis=-1, keepdims=True)
            out = d * lax.rsqrt(var + LN_EPS) * g_ref[...] + b_ref[...]
            of_ref[r, :] = out
            ob_ref[r, :] = out.astype(BF16)
            return 0

        lax.fori_loop(0, tm // rc, rows, 0)


def _matmul_ln(a, w, x, g, b, layer, *, tm, tk):
    m, k = a.shape
    n = w.shape[2]
    once = pl.Buffered(1)
    return pl.pallas_call(
        _mm_ln_kernel,
        grid=(m // tm, k // tk),
        in_specs=[
            pl.BlockSpec((tm, tk), lambda i, kk: (i, kk)),
            pl.BlockSpec((None, tk, n), lambda i, kk: (layer, kk, 0)),
            pl.BlockSpec((tm, n), lambda i, kk: (i, 0), pipeline_mode=once),
            pl.BlockSpec((None, 1, n), lambda i, kk: (layer, 0, 0)),
            pl.BlockSpec((None, 1, n), lambda i, kk: (layer, 0, 0)),
        ],
        out_specs=[
            pl.BlockSpec((tm, n), lambda i, kk: (i, 0), pipeline_mode=once),
            pl.BlockSpec((tm, n), lambda i, kk: (i, 0), pipeline_mode=once),
        ],
        out_shape=[jax.ShapeDtypeStruct((m, n), F32), jax.ShapeDtypeStruct((m, n), BF16)],
        compiler_params=_params("parallel", "arbitrary"),
        name="matmul_ln",
    )(a, w, x, g.reshape(g.shape[0], 1, n), b.reshape(b.shape[0], 1, n))


def _ffn_in_kernel(x_ref, wg_ref, wu_ref, o_ref):
    x = x_ref[...]
    g = _dot(x, wg_ref[...].astype(BF16))
    u = _dot(x, wu_ref[...].astype(BF16))
    o_ref[...] = (jax.nn.silu(g) * u).astype(o_ref.dtype)


def _ffn_in(x, w, layer, *, tm, tf):
    m, k = x.shape
    nf = D_FF // tf
    return pl.pallas_call(
        _ffn_in_kernel,
        grid=(m // tm, nf),
        in_specs=[
            pl.BlockSpec((tm, k), lambda i, j: (i, 0), pipeline_mode=pl.Buffered(1)),
            pl.BlockSpec((None, k, tf), lambda i, j: (layer, 0, j)),
            pl.BlockSpec((None, k, tf), lambda i, j: (layer, 0, j + nf)),
        ],
        out_specs=pl.BlockSpec((tm, tf), lambda i, j: (i, j)),
        out_shape=jax.ShapeDtypeStruct((m, D_FF), BF16),
        compiler_params=_params("parallel", "arbitrary"),
        name="ffn_in",
    )(x, w, w)


def _s5_kernel(u_ref, bb_ref, cb_ref, are_ref, aim_ref, apre_ref, apim_ref, d_ref, s0re_ref, s0im_ref,
               y_ref, sre_ref, sim_ref,
               uslab, uperm, bu, yslab, cre, cim, *, seg, pitch):
    half = S5_HALF

    @pl.when(pl.program_id(2) == 0)
    def _():
        cre[...] = s0re_ref[...]
        cim[...] = s0im_ref[...]

    if seg == 1:
        uperm[...] = u_ref[...]
    else:
        for i in range(SUBLANES):
            for j in range(S5_CH // LANES):
                uslab[j, i * pitch:i * pitch + seg, :] = u_ref[i * seg:(i + 1) * seg, j * LANES:(j + 1) * LANES]

        def gather(r, _):
            row = pl.ds(pl.multiple_of(r * SUBLANES, SUBLANES), SUBLANES)
            for j in range(S5_CH // LANES):
                uperm[row, j * LANES:(j + 1) * LANES] = uslab[j, pl.ds(r, SUBLANES, stride=pitch), :]
            return 0

        lax.fori_loop(0, seg, gather, 0)

    up = uperm[...]
    bu[...] = _dot(up.astype(BF16), bb_ref[...])

    are = jnp.broadcast_to(are_ref[...], (SUBLANES, half))
    aim = jnp.broadcast_to(aim_ref[...], (SUBLANES, half))

    def advance(r, sre, sim):
        row = pl.ds(pl.multiple_of(r * SUBLANES, SUBLANES), SUBLANES)
        nre = are * sre - aim * sim + bu[row, :half]
        nim = are * sim + aim * sre + bu[row, half:]
        return row, nre, nim

    def local_step(r, carry):
        _, nre, nim = advance(r, *carry)
        return nre, nim

    zero = jnp.zeros((SUBLANES, half), F32)
    ere, eim = lax.fori_loop(0, seg, local_step, (zero, zero))

    apre = apre_ref[...]
    apim = apim_ref[...]
    rows_re = [cre[...]]
    rows_im = [cim[...]]
    for i in range(1, SUBLANES + 1):
        pre, pim = rows_re[-1], rows_im[-1]
        rows_re.append(ere[i - 1:i] + apre * pre - apim * pim)
        rows_im.append(eim[i - 1:i] + apre * pim + apim * pre)
    cre[...] = rows_re[-1]
    cim[...] = rows_im[-1]
    sre_ref[...] = rows_re[-1]
    sim_ref[...] = rows_im[-1]
    cin_re = jnp.concatenate(rows_re[:SUBLANES], axis=0)
    cin_im = jnp.concatenate(rows_im[:SUBLANES], axis=0)

    def true_step(r, carry):
        row, nre, nim = advance(r, *carry)
        bu[row, :half] = nre
        bu[row, half:] = nim
        return nre, nim

    lax.fori_loop(0, seg, true_step, (cin_re, cin_im))

    yp = _dot(bu[...].astype(BF16), cb_ref[...]) + d_ref[...] * up
    yp = jax.nn.gelu(yp)
    if seg == 1:
        y_ref[...] = yp
    else:
        for j in range(S5_CH // LANES):
            yslab[j] = yp[:, j * LANES:(j + 1) * LANES]
        tiles = seg // SUBLANES

        def scatter(t, _):
            i = t // tiles
            q = t % tiles
            dst = pl.ds(pl.multiple_of(i * seg + q * SUBLANES, SUBLANES), SUBLANES)
            for j in range(S5_CH // LANES):
                y_ref[dst, j * LANES:(j + 1) * LANES] = yslab[j, pl.ds(q * SUBLANES * SUBLANES + i, SUBLANES,
                                                                    stride=SUBLANES), :]
            return 0

        lax.fori_loop(0, SUBLANES * tiles, scatter, 0)


def _s5_tables(lam_re, lam_im, log_dt, b_re, b_im, c_re, c_im, d_skip, seg):
    dt = jnp.exp(log_dt)[:, None]
    mag = jnp.exp(lam_re * dt)
    ab_re = mag * jnp.cos(lam_im * dt)
    ab_im = mag * jnp.sin(lam_im * dt)
    den = lam_re * lam_re + lam_im * lam_im
    nr = ab_re - 1.0
    f_re = (nr * lam_re + ab_im * lam_im) / den
    f_im = (ab_im * lam_re - nr * lam_im) / den
    bb_re = f_re[..., None] * b_re - f_im[..., None] * b_im
    bb_im = f_re[..., None] * b_im + f_im[..., None] * b_re
    eye = jnp.eye(S5_GROUPS_PER_BLOCK, dtype=F32)
    gpb = S5_GROUPS_PER_BLOCK

    def expand_b(bb):
        bb = bb.reshape(S5_BLOCKS, gpb, SSM_STATE, SSM_GROUP)
        return jnp.einsum('bgpc,gh->bgchp', bb, eye).reshape(S5_BLOCKS, S5_CH, S5_HALF)

    def expand_c(cc):
        cc = cc.reshape(S5_BLOCKS, gpb, SSM_GROUP, SSM_STATE)
        return jnp.einsum('bgcp,gh->bgphc', cc, eye).reshape(S5_BLOCKS, S5_HALF, S5_CH)

    bblk = jnp.concatenate([expand_b(bb_re), expand_b(bb_im)], axis=2).astype(BF16)
    cblk = jnp.concatenate([expand_c(c_re), -expand_c(c_im)], axis=1).astype(BF16)
    ap_re, ap_im = ab_re, ab_im
    for _ in range(int(math.log2(seg))):
        ap_re, ap_im = ap_re * ap_re - ap_im * ap_im, 2.0 * ap_re * ap_im
    vec = lambda v: v.reshape(S5_BLOCKS, 1, S5_HALF)
    return (bblk, cblk, vec(ab_re), vec(ab_im), vec(ap_re), vec(ap_im),
            d_skip.reshape(S5_BLOCKS, 1, S5_CH))


def _s5(h, tables, s0_re, s0_im, batch, slen):
    bblk, cblk, are, aim, apre, apim, dsk = tables
    seg = S5_SEG if slen >= SUBLANES * S5_SEG else slen // SUBLANES
    tc = SUBLANES * seg
    nchunk = slen // tc
    pitch = S5_PITCH
    vec = pl.BlockSpec((None, 1, S5_HALF), lambda b, g, c: (g, 0, 0))
    state = pl.BlockSpec((None, None, 1, S5_HALF), lambda b, g, c: (b, g, 0, 0))
    y, sre, sim = pl.pallas_call(
        functools.partial(_s5_kernel, seg=seg, pitch=pitch),
        grid=(batch, S5_BLOCKS, nchunk),
        in_specs=[
            pl.BlockSpec((tc, S5_CH), lambda b, g, c: (b * nchunk + c, g)),
            pl.BlockSpec((None, S5_CH, 2 * S5_HALF), lambda b, g, c: (g, 0, 0)),
            pl.BlockSpec((None, 2 * S5_HALF, S5_CH), lambda b, g, c: (g, 0, 0)),
            vec, vec, vec, vec,
            pl.BlockSpec((None, 1, S5_CH), lambda b, g, c: (g, 0, 0)),
            state, state,
        ],
        out_specs=[
            pl.BlockSpec((tc, S5_CH), lambda b, g, c: (b * nchunk + c, g)),
            state, state,
        ],
        out_shape=[
            jax.ShapeDtypeStruct((batch * slen, MIX_WIDTH), F32),
            jax.ShapeDtypeStruct((batch, S5_BLOCKS, 1, S5_HALF), F32),
            jax.ShapeDtypeStruct((batch, S5_BLOCKS, 1, S5_HALF), F32),
        ],
        scratch_shapes=[
            pltpu.VMEM((S5_CH // LANES, SUBLANES * pitch, LANES), F32),
            pltpu.VMEM((tc, S5_CH), F32),
            pltpu.VMEM((tc, 2 * S5_HALF), F32),
            pltpu.VMEM((S5_CH // LANES, tc, LANES), F32),
            pltpu.VMEM((1, S5_HALF), F32),
            pltpu.VMEM((1, S5_HALF), F32),
        ],
        compiler_params=_params("parallel", "parallel", "arbitrary"),
        name="s5_mixer",
    )(h, bblk, cblk, are, aim, apre, apim, dsk, s0_re, s0_im)
    return y, sre, sim


def _diff_lambda(lq1, lk1, lq2, lk2, lam_init):
    e1 = jnp.exp(jnp.sum(lq1[...] * lk1[...], axis=-1, keepdims=True))
    e2 = jnp.exp(jnp.sum(lq2[...] * lk2[...], axis=-1, keepdims=True))
    return e1 - e2 + lam_init


def _softmax_update(s, v, m, l, acc):
    m_new = jnp.maximum(m, jnp.max(s, axis=-1, keepdims=True))
    alpha = jnp.exp(m - m_new)
    p = jnp.exp(s - m_new)
    l_new = alpha * l + jnp.sum(p, axis=-1, keepdims=True)
    acc_new = alpha * acc + _dot(p.astype(BF16), v)
    return m_new, l_new, acc_new


def _head_norm(o, g, lam_init):
    o = o * lax.rsqrt(jnp.mean(o * o, axis=-1, keepdims=True) + LN_EPS) * g
    return o * (1.0 - lam_init)


def _dattn_prompt_kernel(lq1, lk1, lq2, lk2, g_ref, q_ref, k_ref, v_ref, o_ref, k1s, k2s, vs, *, tq, lam_init):
    qi = pl.program_id(2)
    d = DIFF_HEAD_DIM
    scale = d ** -0.5

    @pl.when(qi == 0)
    def _():
        k1s[...] = k_ref[:, :d].astype(BF16)
        k2s[...] = k_ref[:, d:].astype(BF16)
        vs[...] = v_ref[...].astype(BF16)

    lam = _diff_lambda(lq1, lk1, lq2, lk2, lam_init)
    q1 = q_ref[:, :d].astype(BF16)
    q2 = q_ref[:, d:].astype(BF16)

    def block(j):
        rows = pl.ds(pl.multiple_of(j * tq, tq), tq)
        s1 = _dot_nt(q1, k1s[rows, :]) * scale
        s2 = _dot_nt(q2, k2s[rows, :]) * scale
        return s1, s2, vs[rows, :]

    def body(j, st):
        s1, s2, v = block(j)
        return _softmax_update(s1, v, *st[:3]) + _softmax_update(s2, v, *st[3:])

    init = (jnp.full((tq, 1), -jnp.inf, F32), jnp.zeros((tq, 1), F32), jnp.zeros((tq, DIFF_V_DIM), F32))
    st = lax.fori_loop(0, qi, body, init + init)

    s1, s2, v = block(qi)
    causal = lax.broadcasted_iota(jnp.int32, (tq, tq), 0) >= lax.broadcasted_iota(jnp.int32, (tq, tq), 1)
    s1 = jnp.where(causal, s1, -jnp.inf)
    s2 = jnp.where(causal, s2, -jnp.inf)
    _, l1, a1 = _softmax_update(s1, v, *st[:3])
    _, l2, a2 = _softmax_update(s2, v, *st[3:])
    o = a1 / l1 - lam * (a2 / l2)
    o_ref[...] = _head_norm(o, g_ref[...], lam_init).astype(o_ref.dtype)


def _row3(v):
    return v.reshape(v.shape[0], 1, v.shape[1])


def _dattn_prompt(h, w, layer, lam_init, batch, slen, *, tq):
    nq = slen // tq
    lam_spec = pl.BlockSpec((None, 1, DIFF_HEAD_DIM), lambda b, hh, qi: (layer, 0, 0))
    row3 = _row3
    return pl.pallas_call(
        functools.partial(_dattn_prompt_kernel, tq=tq, lam_init=lam_init),
        grid=(batch, DIFF_HEADS, nq),
        in_specs=[
            lam_spec, lam_spec, lam_spec, lam_spec,
            pl.BlockSpec((None, 1, DIFF_V_DIM), lambda b, hh, qi: (layer, 0, 0)),
            pl.BlockSpec((tq, DIFF_V_DIM), lambda b, hh, qi: (b * nq + qi, hh)),
            pl.BlockSpec((slen, DIFF_V_DIM), lambda b, hh, qi: (b, DIFF_HEADS + hh)),
            pl.BlockSpec((slen, DIFF_V_DIM), lambda b, hh, qi: (b, 2 * DIFF_HEADS + hh)),
        ],
        out_specs=pl.BlockSpec((tq, DIFF_V_DIM), lambda b, hh, qi: (b * nq + qi, hh)),
        out_shape=jax.ShapeDtypeStruct((batch * slen, MIX_WIDTH), BF16),
        scratch_shapes=[
            pltpu.VMEM((slen, DIFF_HEAD_DIM), BF16),
            pltpu.VMEM((slen, DIFF_HEAD_DIM), BF16),
            pltpu.VMEM((slen, DIFF_V_DIM), BF16),
        ],
        compiler_params=_params("parallel", "parallel", "arbitrary"),
        name="diff_attn_prompt",
    )(row3(w['diff_lambda_q1']), row3(w['diff_lambda_k1']), row3(w['diff_lambda_q2']), row3(w['diff_lambda_k2']),
      row3(w['diff_subln_g']), h, h, h)


def _dattn_decode_kernel(pt_ref, lq1, lk1, lq2, lk2, g_ref, q_ref, kn_ref, vn_ref, kc_ref, vc_ref, o_ref,
                         m_s, l_s, acc_s, *, tnew, lam_init):
    del pt_ref
    p = pl.program_id(1)
    d = DIFF_HEAD_DIM
    scale = d ** -0.5
    kv_rows = DIFF_HEADS * 2

    @pl.when(p == 0)
    def _():
        m_s[...] = jnp.full_like(m_s, -jnp.inf)
        l_s[...] = jnp.zeros_like(l_s)
        acc_s[...] = jnp.zeros_like(acc_s)

    def q_pair(hh):
        return [q_ref[:, (2 * hh + c) * d:(2 * hh + c + 1) * d].astype(BF16) for c in range(2)]

    def update(hh, s, v):
        m, l, acc = _softmax_update(s, v, m_s[hh], l_s[hh], acc_s[hh])
        m_s[hh] = m
        l_s[hh] = l
        acc_s[hh] = acc

    for hh in range(DIFF_HEADS):
        q1, q2 = q_pair(hh)
        k1 = kc_ref[pl.ds(2 * hh, PAGE_SIZE, stride=kv_rows), :].astype(BF16)
        k2 = kc_ref[pl.ds(2 * hh + 1, PAGE_SIZE, stride=kv_rows), :].astype(BF16)
        s = jnp.concatenate([_dot_nt(q1, k1), _dot_nt(q2, k2)], axis=0) * scale
        update(hh, s, vc_ref[hh].astype(BF16))

    @pl.when(p == pl.num_programs(1) - 1)
    def _():
        lam = _diff_lambda(lq1, lk1, lq2, lk2, lam_init)
        pad_rows = jnp.zeros((LANES - tnew, d), BF16)
        pad_v = jnp.zeros((LANES - tnew, DIFF_V_DIM), BF16)
        row = lax.broadcasted_iota(jnp.int32, (tnew, LANES), 0)
        col = lax.broadcasted_iota(jnp.int32, (tnew, LANES), 1)
        causal = jnp.concatenate([col <= row, col <= row], axis=0)
        for hh in range(DIFF_HEADS):
            q1, q2 = q_pair(hh)
            k1 = jnp.concatenate([kn_ref[:, (2 * hh) * d:(2 * hh + 1) * d].astype(BF16), pad_rows], axis=0)
            k2 = jnp.concatenate([kn_ref[:, (2 * hh + 1) * d:(2 * hh + 2) * d].astype(BF16), pad_rows], axis=0)
            v = jnp.concatenate([vn_ref[:, hh * DIFF_V_DIM:(hh + 1) * DIFF_V_DIM].astype(BF16), pad_v], axis=0)
            s = jnp.concatenate([_dot_nt(q1, k1), _dot_nt(q2, k2)], axis=0) * scale
            update(hh, jnp.where(causal, s, -jnp.inf), v)
            acc = acc_s[hh] / l_s[hh]
            o = acc[:tnew] - lam * acc[tnew:]
            o_ref[:, hh * DIFF_V_DIM:(hh + 1) * DIFF_V_DIM] = _head_norm(o, g_ref[...], lam_init).astype(o_ref.dtype)


def _dattn_decode(h, k_pool, v_pool, page_table, w, layer, lam_init, batch, tnew):
    n_pool = k_pool.shape[1]
    n_pages = page_table.shape[1]
    kv_rows = DIFF_HEADS * 2
    kc = k_pool[layer].reshape(n_pool * PAGE_SIZE * kv_rows, DIFF_HEAD_DIM)
    vc = jnp.transpose(v_pool[layer], (0, 2, 1, 3))
    lam_spec = pl.BlockSpec((None, 1, DIFF_HEAD_DIM), lambda b, p, pt: (layer, 0, 0))
    row3 = _row3
    new_spec = lambda blk: pl.BlockSpec((tnew, MIX_WIDTH), lambda b, p, pt: (b, blk))
    grid_spec = pltpu.PrefetchScalarGridSpec(
        num_scalar_prefetch=1,
        grid=(batch, n_pages),
        in_specs=[
            lam_spec, lam_spec, lam_spec, lam_spec,
            pl.BlockSpec((None, 1, DIFF_V_DIM), lambda b, p, pt: (layer, 0, 0)),
            new_spec(0), new_spec(1), new_spec(2),
            pl.BlockSpec((PAGE_SIZE * kv_rows, DIFF_HEAD_DIM), lambda b, p, pt: (pt[b, p], 0)),
            pl.BlockSpec((None, DIFF_HEADS, PAGE_SIZE, DIFF_V_DIM), lambda b, p, pt: (pt[b, p], 0, 0, 0)),
        ],
        out_specs=pl.BlockSpec((tnew, MIX_WIDTH), lambda b, p, pt: (b, 0)),
        scratch_shapes=[
            pltpu.VMEM((DIFF_HEADS, 2 * tnew, 1), F32),
            pltpu.VMEM((DIFF_HEADS, 2 * tnew, 1), F32),
            pltpu.VMEM((DIFF_HEADS, 2 * tnew, DIFF_V_DIM), F32),
        ],
    )
    return pl.pallas_call(
        functools.partial(_dattn_decode_kernel, tnew=tnew, lam_init=lam_init),
        grid_spec=grid_spec,
        out_shape=jax.ShapeDtypeStruct((batch * tnew, MIX_WIDTH), _act_dtype(tnew)),
        compiler_params=_params("parallel", "arbitrary"),
        name="diff_attn_decode",
    )(page_table, row3(w['diff_lambda_q1']), row3(w['diff_lambda_k1']), row3(w['diff_lambda_q2']),
      row3(w['diff_lambda_k2']), row3(w['diff_subln_g']), h, h, h, kc, vc)


def _mem_attn_kernel(q_ref, mk_ref, mv_ref, o_ref):
    s = _dot_nt(q_ref[...].astype(BF16), mk_ref[...].astype(BF16)) * (MEM_HEAD_DIM ** -0.5)
    p = jnp.exp(s - jnp.max(s, axis=-1, keepdims=True))
    l = jnp.sum(p, axis=-1, keepdims=True)
    o_ref[...] = (_dot(p.astype(BF16), mv_ref[...].astype(BF16)) / l).astype(o_ref.dtype)


def _mem_attn(h, q_col, mem_kv, batch, slen, *, ts):
    ns = slen // ts
    return pl.pallas_call(
        _mem_attn_kernel,
        grid=(batch, ns, MEM_HEADS),
        in_specs=[
            pl.BlockSpec((ts, MEM_HEAD_DIM), lambda b, si, hh: (b * ns + si, q_col + hh)),
            pl.BlockSpec((MEM_TOKENS, MEM_HEAD_DIM), lambda b, si, hh: (b, hh)),
            pl.BlockSpec((MEM_TOKENS, MEM_HEAD_DIM), lambda b, si, hh: (b, MEM_HEADS + hh)),
        ],
        out_specs=pl.BlockSpec((ts, MEM_HEAD_DIM), lambda b, si, hh: (b * ns + si, hh)),
        out_shape=jax.ShapeDtypeStruct((batch * slen, MEM_WIDTH), _act_dtype(ts)),
        compiler_params=_params("parallel", "parallel", "parallel"),
        name="mem_attn",
    )(h, mem_kv, mem_kv)


def _tiles(m):
    if m >= 2048:
        return dict(mm=dict(tm=2048, tn=1024, tk=1024), ln=dict(tm=1024, tk=256), ffn=dict(tm=1024, tf=256),
                    glu=dict(tm=1024, tn=1024, tk=1024))
    return dict(mm=dict(tm=m, tn=1024, tk=1024), ln=dict(tm=m, tk=256), ffn=dict(tm=m, tf=256),
                glu=dict(tm=m, tn=1024, tk=1024))


def _run_trunk(x, batch, slen, mem_kv, ssm_state, attn_past, w):
    m = batch * slen
    t = _tiles(m)
    ts = min(slen, 512)
    new_k = new_v = s_re = s_im = None
    x_f32, x_in = x, x
    for i in range(DEPTH):
        li = i // 2
        if i % 2 == 0:
            h = _matmul(x_in, w['w_in_ssm'], li, **t['mm'])
            seg = S5_SEG if slen >= SUBLANES * S5_SEG else slen // SUBLANES
            tables = _s5_tables(w['ssm_lambda_re'][li], w['ssm_lambda_im'][li], w['ssm_log_dt'][li],
                                w['ssm_b_re'][li], w['ssm_b_im'][li], w['ssm_c_re'][li], w['ssm_c_im'][li],
                                w['ssm_d'][li], seg)
            if ssm_state is None:
                s0_re = s0_im = jnp.zeros((batch, S5_BLOCKS, 1, S5_HALF), F32)
            else:
                s0_re = ssm_state[0][li].reshape(batch, S5_BLOCKS, 1, S5_HALF)
                s0_im = ssm_state[1][li].reshape(batch, S5_BLOCKS, 1, S5_HALF)
            y, s_re, s_im = _s5(h, tables, s0_re, s0_im, batch, slen)
            mix = _glu(y, w['w_glu'], w['b_glu'], li, _act_dtype(ts), **t['glu'])
            q_col = MIX_WIDTH // MEM_HEAD_DIM
        else:
            h = _matmul(x_in, w['w_in_attn'], li, **t['mm'])
            lam_init = 0.8 - 0.6 * math.exp(-0.3 * i)
            if attn_past is None:
                mix = _dattn_prompt(h, w, li, lam_init, batch, slen, tq=256)
            else:
                mix = _dattn_decode(h, attn_past[0], attn_past[1], attn_past[2], w, li, lam_init, batch, slen)
            new_k = h[:, MIX_WIDTH:2 * MIX_WIDTH]
            new_v = h[:, 2 * MIX_WIDTH:3 * MIX_WIDTH]
            q_col = 3 * MIX_WIDTH // MEM_HEAD_DIM
        mem = _mem_attn(h, q_col, mem_kv[i], batch, slen, ts=ts)
        cat = jnp.concatenate([mix, mem], axis=-1)
        x_f32, x_bf = _matmul_ln(cat, w['w_o'], x_f32, w['ln1_g'], w['ln1_b'], i, **t['ln'])
        hid = _ffn_in(x_bf, w['w_ffn_in'], i, **t['ffn'])
        x_f32, x_in = _matmul_ln(hid, w['w_ffn_out'], x_f32, w['ln2_g'], w['ln2_b'], i, **t['ln'])
    return x_f32, new_k, new_v, s_re, s_im


def kernel(x_prompt, x_sample, cache_attn_k, cache_attn_v, state_ssm_re, state_ssm_im, cache_mem_k, cache_mem_v,
           page_table, mem_prompt, w_in_ssm, ssm_lambda_re, ssm_lambda_im, ssm_log_dt, ssm_b_re, ssm_b_im,
           ssm_c_re, ssm_c_im, ssm_d, w_glu, b_glu, w_in_attn, diff_lambda_q1, diff_lambda_k1, diff_lambda_q2,
           diff_lambda_k2, diff_subln_g, w_mem_kv, w_o, ln1_g, ln1_b, w_ffn_in, w_ffn_out, ln2_g, ln2_b):
    w = dict(w_in_ssm=w_in_ssm, ssm_lambda_re=ssm_lambda_re, ssm_lambda_im=ssm_lambda_im, ssm_log_dt=ssm_log_dt,
             ssm_b_re=ssm_b_re, ssm_b_im=ssm_b_im, ssm_c_re=ssm_c_re, ssm_c_im=ssm_c_im, ssm_d=ssm_d,
             w_glu=w_glu, b_glu=b_glu, w_in_attn=w_in_attn, diff_lambda_q1=diff_lambda_q1,
             diff_lambda_k1=diff_lambda_k1, diff_lambda_q2=diff_lambda_q2, diff_lambda_k2=diff_lambda_k2,
             diff_subln_g=diff_subln_g, w_o=w_o, ln1_g=ln1_g, ln1_b=ln1_b, w_ffn_in=w_ffn_in,
             w_ffn_out=w_ffn_out, ln2_g=ln2_g, ln2_b=ln2_b)
    bp, sp, _ = x_prompt.shape
    bs, ss, _ = x_sample.shape

    mem_rows = mem_prompt.reshape(bp * MEM_TOKENS, D_MODEL)
    mem_kv_p = [_matmul(mem_rows, w_mem_kv, l, tm=bp * MEM_TOKENS, tn=1024, tk=1024) for l in range(DEPTH)]
    mem_k_prompt = jnp.stack([kv[:, :MEM_WIDTH].reshape(bp, MEM_TOKENS, MEM_HEADS, MEM_HEAD_DIM) for kv in mem_kv_p])
    mem_v_prompt = jnp.stack([kv[:, MEM_WIDTH:].reshape(bp, MEM_TOKENS, MEM_HEADS, MEM_HEAD_DIM) for kv in mem_kv_p])
    mem_kv_s = [jnp.concatenate([cache_mem_k[l].reshape(bs * MEM_TOKENS, MEM_WIDTH),
                                 cache_mem_v[l].reshape(bs * MEM_TOKENS, MEM_WIDTH)], axis=1) for l in range(DEPTH)]

    yp, kp, vp, srp, sip = _run_trunk(x_prompt.reshape(bp * sp, D_MODEL), bp, sp, mem_kv_p, None, None, w)
    ys, ks, vs, srs, sis = _run_trunk(x_sample.reshape(bs * ss, D_MODEL), bs, ss, mem_kv_s,
                                      (state_ssm_re, state_ssm_im), (cache_attn_k, cache_attn_v, page_table), w)

    def kshape(k, b, s):
        return k.reshape(1, b, s, DIFF_HEADS, 2, DIFF_HEAD_DIM)

    def vshape(v, b, s):
        return v.reshape(1, b, s, DIFF_HEADS, DIFF_V_DIM)

    def sshape(s, b):
        return s.reshape(1, b, SSM_GROUPS, SSM_STATE)

    return (yp.reshape(bp, sp, D_MODEL), ys.reshape(bs, ss, D_MODEL),
            kshape(kp, bp, sp), vshape(vp, bp, sp), kshape(ks, bs, ss), vshape(vs, bs, ss),
            sshape(srp, bp), sshape(sip, bp), sshape(srs, bs), sshape(sis, bs),
            mem_k_prompt, mem_v_prompt)
```

```python
import functools
import math

import jax
import jax.numpy as jnp
from jax import lax
from jax.experimental import pallas as pl
from jax.experimental.pallas import tpu as pltpu

F32 = jnp.float32
BF16 = jnp.bfloat16

D_MODEL = 4096
DEPTH = 2
MEM_TOKENS = 256
MEM_HEADS = 4
MEM_HEAD_DIM = 256
MEM_WIDTH = MEM_HEADS * MEM_HEAD_DIM
MIX_WIDTH = D_MODEL - MEM_WIDTH
SSM_GROUP = 16
SSM_GROUPS = MIX_WIDTH // SSM_GROUP
SSM_STATE = 64
DIFF_HEAD_DIM = 128
DIFF_HEADS = MIX_WIDTH // (2 * DIFF_HEAD_DIM)
DIFF_V_DIM = 2 * DIFF_HEAD_DIM
D_FF = 11008
ALPHA = (2 * DEPTH) ** 0.25
LN_EPS = 1e-5
PAGE_SIZE = 128

VMEM_LIMIT_BYTES = 58 * 1024 * 1024
LANES = 128
SUBLANES = 8

S5_GROUPS_PER_BLOCK = 16
S5_CH = S5_GROUPS_PER_BLOCK * SSM_GROUP
S5_HALF = S5_GROUPS_PER_BLOCK * SSM_STATE
S5_BLOCKS = SSM_GROUPS // S5_GROUPS_PER_BLOCK
S5_SEG = 64
S5_PITCH = S5_SEG + SUBLANES
LN_ROWS = 128
DECODE_PAGES = 2
PROMPT_Q_BLOCK = 512


def _params(*sem):
    return pltpu.CompilerParams(dimension_semantics=sem, vmem_limit_bytes=VMEM_LIMIT_BYTES)


def _dot(a, b):
    return jnp.dot(a, b, preferred_element_type=F32)


def _dot_nt(a, b):
    return lax.dot_general(a, b, (((1,), (1,)), ((), ())), preferred_element_type=F32)


def _w_spec(w, layer, block, index):
    if w.ndim == 3:
        return pl.BlockSpec((None,) + block, lambda *g: (layer,) + index(*g))
    return pl.BlockSpec(block, index)


def _accumulate(ref, first, term):
    @pl.when(first)
    def _():
        ref[...] = term()

    @pl.when(jnp.logical_not(first))
    def _():
        ref[...] += term()


def _mm_kernel(x_ref, w_ref, o_ref, *wb_ref):
    w = w_ref[...].astype(BF16)
    if wb_ref:
        wb_ref[0][...] = w
    _accumulate(o_ref, pl.program_id(2) == 0, lambda: _dot(x_ref[...].astype(BF16), w))


def _matmul(x, w, layer=None, *, tm, tn, tk, emit=False):
    m, k = x.shape
    n = w.shape[-1]
    assert not emit or m == tm
    out_specs = [pl.BlockSpec((tm, tn), lambda i, j, kk: (i, j))]
    out_shape = [jax.ShapeDtypeStruct((m, n), F32)]
    if emit:
        out_specs.append(pl.BlockSpec((tk, tn), lambda i, j, kk: (kk, j)))
        out_shape.append(jax.ShapeDtypeStruct((k, n), BF16))
    res = pl.pallas_call(
        _mm_kernel,
        grid=(m // tm, n // tn, k // tk),
        in_specs=[
            pl.BlockSpec((tm, tk), lambda i, j, kk: (i, kk)),
            _w_spec(w, layer, (tk, tn), lambda i, j, kk: (kk, j)),
        ],
        out_specs=out_specs,
        out_shape=out_shape,
        compiler_params=_params("parallel", "parallel", "arbitrary"),
        name="matmul",
    )(x, w)
    return res if emit else res[0]


def _glu_kernel(yk_ref, w_ref, b_ref, yn_ref, o_ref, *rest):
    acc_ref = rest[-1]
    kk = pl.program_id(2)
    w = w_ref[...].astype(BF16)
    if len(rest) == 2:
        rest[0][...] = w
    _accumulate(acc_ref, kk == 0, lambda: _dot(yk_ref[...].astype(BF16), w))

    @pl.when(kk == pl.num_programs(2) - 1)
    def _():
        gate = jax.nn.sigmoid(acc_ref[...] + b_ref[...])
        o_ref[...] = (yn_ref[...] * gate).astype(o_ref.dtype)


def _act_dtype(rows_per_block):
    return BF16 if rows_per_block % (2 * SUBLANES) == 0 else F32


def _glu(y, w, b, layer, out_dtype, *, tm, tn, tk, emit=False):
    m, k = y.shape
    n = w.shape[-1]
    assert not emit or m == tm
    out_specs = [pl.BlockSpec((tm, tn), lambda i, j, kk: (i, j))]
    out_shape = [jax.ShapeDtypeStruct((m, n), out_dtype)]
    if emit:
        out_specs.append(pl.BlockSpec((tk, tn), lambda i, j, kk: (kk, j)))
        out_shape.append(jax.ShapeDtypeStruct((k, n), BF16))
    res = pl.pallas_call(
        _glu_kernel,
        grid=(m // tm, n // tn, k // tk),
        in_specs=[
            pl.BlockSpec((tm, tk), lambda i, j, kk: (i, kk)),
            _w_spec(w, layer, (tk, tn), lambda i, j, kk: (kk, j)),
            pl.BlockSpec((None, 1, tn), lambda i, j, kk: (layer, 0, j)),
            pl.BlockSpec((tm, tn), lambda i, j, kk: (i, j)),
        ],
        out_specs=out_specs,
        out_shape=out_shape,
        scratch_shapes=[pltpu.VMEM((tm, tn), F32)],
        compiler_params=_params("parallel", "parallel", "arbitrary"),
        name="glu",
    )(y, w, b.reshape(b.shape[0], 1, b.shape[1]), y)
    return res if emit else res[0]


def _mm_ln_kernel(a_ref, w_ref, x_ref, g_ref, b_ref, of_ref, ob_ref, *rest):
    acc = rest[-1]
    j = pl.program_id(1)
    w = w_ref[...].astype(BF16)
    if len(rest) == 2:
        rest[0][...] = w
    acc[j] = _dot(a_ref[...].astype(BF16), w)

    @pl.when(j == pl.num_programs(1) - 1)
    def _():
        panels, tm, tn = acc.shape
        rc = min(tm, LN_ROWS)
        width = panels * tn

        def rows(c, _):
            r = pl.ds(pl.multiple_of(c * rc, rc), rc)
            cols = [slice(p * tn, (p + 1) * tn) for p in range(panels)]
            ys = [ALPHA * x_ref[r, cs] + acc[p, r, :] for p, cs in enumerate(cols)]
            mu = sum(jnp.sum(y, axis=-1, keepdims=True) for y in ys) / width
            ds = [y - mu for y in ys]
            var = sum(jnp.sum(d * d, axis=-1, keepdims=True) for d in ds) / width
            inv = lax.rsqrt(var + LN_EPS)
            for d, cs in zip(ds, cols):
                out = d * inv * g_ref[:, cs] + b_ref[:, cs]
                of_ref[r, cs] = out
                ob_ref[r, cs] = out.astype(BF16)
            return 0

        lax.fori_loop(0, tm // rc, rows, 0)


def _matmul_ln(a, w, x, g, b, layer, *, tm, tn, emit=False):
    m, k = a.shape
    n = w.shape[-1]
    assert not emit or m == tm
    once = pl.Buffered(1)
    out_specs = [
        pl.BlockSpec((tm, n), lambda i, j: (i, 0), pipeline_mode=once),
        pl.BlockSpec((tm, n), lambda i, j: (i, 0), pipeline_mode=once),
    ]
    out_shape = [jax.ShapeDtypeStruct((m, n), F32), jax.ShapeDtypeStruct((m, n), BF16)]
    if emit:
        out_specs.append(pl.BlockSpec((k, tn), lambda i, j: (0, j)))
        out_shape.append(jax.ShapeDtypeStruct((k, n), BF16))
    return pl.pallas_call(
        _mm_ln_kernel,
        grid=(m // tm, n // tn),
        in_specs=[
            pl.BlockSpec((tm, k), lambda i, j: (i, 0), pipeline_mode=once),
            _w_spec(w, layer, (k, tn), lambda i, j: (0, j)),
            pl.BlockSpec((tm, n), lambda i, j: (i, 0), pipeline_mode=once),
            pl.BlockSpec((None, 1, n), lambda i, j: (layer, 0, 0)),
            pl.BlockSpec((None, 1, n), lambda i, j: (layer, 0, 0)),
        ],
        out_specs=out_specs,
        out_shape=out_shape,
        scratch_shapes=[pltpu.VMEM((n // tn, tm, tn), F32)],
        compiler_params=_params("parallel", "arbitrary"),
        name="matmul_ln",
    )(a, w, x, g.reshape(g.shape[0], 1, n), b.reshape(b.shape[0], 1, n))


def _ffn_in_kernel(x_ref, wg_ref, wu_ref, o_ref, *wb_refs):
    x = x_ref[...]
    wg = wg_ref[...].astype(BF16)
    wu = wu_ref[...].astype(BF16)
    if wb_refs:
        wb_refs[0][...] = wg
        wb_refs[1][...] = wu
    o_ref[...] = (jax.nn.silu(_dot(x, wg)) * _dot(x, wu)).astype(o_ref.dtype)


def _ffn_in(x, w, layer=None, *, tm, tf, emit=False):
    m, k = x.shape
    nf = D_FF // tf
    assert not emit or m == tm
    if isinstance(w, tuple):
        wg, wu = w
        w_specs = [pl.BlockSpec((k, tf), lambda i, j: (0, j)), pl.BlockSpec((k, tf), lambda i, j: (0, j))]
    else:
        wg = wu = w
        w_specs = [pl.BlockSpec((None, k, tf), lambda i, j: (layer, 0, j)),
                   pl.BlockSpec((None, k, tf), lambda i, j: (layer, 0, j + nf))]
    out_specs = [pl.BlockSpec((tm, tf), lambda i, j: (i, j))]
    out_shape = [jax.ShapeDtypeStruct((m, D_FF), BF16)]
    if emit:
        out_specs += [pl.BlockSpec((k, tf), lambda i, j: (0, j))] * 2
        out_shape += [jax.ShapeDtypeStruct((k, D_FF), BF16)] * 2
    res = pl.pallas_call(
        _ffn_in_kernel,
        grid=(m // tm, nf),
        in_specs=[pl.BlockSpec((tm, k), lambda i, j: (i, 0), pipeline_mode=pl.Buffered(1))] + w_specs,
        out_specs=out_specs,
        out_shape=out_shape,
        compiler_params=_params("parallel", "arbitrary"),
        name="ffn_in",
    )(x, wg, wu)
    return (res[0], (res[1], res[2])) if emit else res[0]


def _s5_kernel(u_ref, bb_ref, cb_ref, are_ref, aim_ref, apre_ref, apim_ref, d_ref, s0re_ref, s0im_ref,
               y_ref, sre_ref, sim_ref,
               uslab, uperm, bu, yslab, cre, cim, *, seg, pitch):
    half = S5_HALF

    @pl.when(pl.program_id(2) == 0)
    def _():
        cre[...] = s0re_ref[...]
        cim[...] = s0im_ref[...]

    if seg == 1:
        uperm[...] = u_ref[...]
    else:
        for i in range(SUBLANES):
            for j in range(S5_CH // LANES):
                uslab[j, i * pitch:i * pitch + seg, :] = u_ref[i * seg:(i + 1) * seg, j * LANES:(j + 1) * LANES]

        def gather(r, _):
            row = pl.ds(pl.multiple_of(r * SUBLANES, SUBLANES), SUBLANES)
            for j in range(S5_CH // LANES):
                uperm[row, j * LANES:(j + 1) * LANES] = uslab[j, pl.ds(r, SUBLANES, stride=pitch), :]
            return 0

        lax.fori_loop(0, seg, gather, 0, unroll=min(seg, 4))

    up = uperm[...]
    bu[...] = _dot(up.astype(BF16), bb_ref[...])

    are = jnp.broadcast_to(are_ref[...], (SUBLANES, half))
    aim = jnp.broadcast_to(aim_ref[...], (SUBLANES, half))

    def advance(r, sre, sim):
        row = pl.ds(pl.multiple_of(r * SUBLANES, SUBLANES), SUBLANES)
        nre = are * sre - aim * sim + bu[row, :half]
        nim = are * sim + aim * sre + bu[row, half:]
        return row, nre, nim

    def local_step(r, carry):
        _, nre, nim = advance(r, *carry)
        return nre, nim

    zero = jnp.zeros((SUBLANES, half), F32)
    ere, eim = lax.fori_loop(0, seg, local_step, (zero, zero), unroll=min(seg, 2))

    apre = apre_ref[...]
    apim = apim_ref[...]
    rows_re = [cre[...]]
    rows_im = [cim[...]]
    for i in range(1, SUBLANES + 1):
        pre, pim = rows_re[-1], rows_im[-1]
        rows_re.append(ere[i - 1:i] + apre * pre - apim * pim)
        rows_im.append(eim[i - 1:i] + apre * pim + apim * pre)
    cre[...] = rows_re[-1]
    cim[...] = rows_im[-1]
    sre_ref[...] = rows_re[-1]
    sim_ref[...] = rows_im[-1]
    cin_re = jnp.concatenate(rows_re[:SUBLANES], axis=0)
    cin_im = jnp.concatenate(rows_im[:SUBLANES], axis=0)

    def true_step(r, carry):
        row, nre, nim = advance(r, *carry)
        bu[row, :half] = nre
        bu[row, half:] = nim
        return nre, nim

    lax.fori_loop(0, seg, true_step, (cin_re, cin_im), unroll=min(seg, 2))

    yp = _dot(bu[...].astype(BF16), cb_ref[...]) + d_ref[...] * up
    yp = jax.nn.gelu(yp)
    if seg == 1:
        y_ref[...] = yp
    else:
        for j in range(S5_CH // LANES):
            yslab[j] = yp[:, j * LANES:(j + 1) * LANES]

        def scatter(q, _):
            for i in range(SUBLANES):
                dst = pl.ds(pl.multiple_of(i * seg + q * SUBLANES, SUBLANES), SUBLANES)
                src = pl.ds(q * SUBLANES * SUBLANES + i, SUBLANES, stride=SUBLANES)
                for j in range(S5_CH // LANES):
                    y_ref[dst, j * LANES:(j + 1) * LANES] = yslab[j, src, :]
            return 0

        lax.fori_loop(0, seg // SUBLANES, scatter, 0)


def _s5_tables(lam_re, lam_im, log_dt, b_re, b_im, c_re, c_im, d_skip):
    dt = jnp.exp(log_dt)[:, None]
    mag = jnp.exp(lam_re * dt)
    ab_re = mag * jnp.cos(lam_im * dt)
    ab_im = mag * jnp.sin(lam_im * dt)
    den = lam_re * lam_re + lam_im * lam_im
    nr = ab_re - 1.0
    f_re = (nr * lam_re + ab_im * lam_im) / den
    f_im = (ab_im * lam_re - nr * lam_im) / den
    bb_re = f_re[..., None] * b_re - f_im[..., None] * b_im
    bb_im = f_re[..., None] * b_im + f_im[..., None] * b_re
    eye = jnp.eye(S5_GROUPS_PER_BLOCK, dtype=F32)
    gpb = S5_GROUPS_PER_BLOCK

    def expand_b(bb):
        bb = bb.reshape(S5_BLOCKS, gpb, SSM_STATE, SSM_GROUP)
        return jnp.einsum('bgpc,gh->bgchp', bb, eye).reshape(S5_BLOCKS, S5_CH, S5_HALF)

    def expand_c(cc):
        cc = cc.reshape(S5_BLOCKS, gpb, SSM_GROUP, SSM_STATE)
        return jnp.einsum('bgcp,gh->bgphc', cc, eye).reshape(S5_BLOCKS, S5_HALF, S5_CH)

    bblk = jnp.concatenate([expand_b(bb_re), expand_b(bb_im)], axis=2).astype(BF16)
    cblk = jnp.concatenate([expand_c(c_re), -expand_c(c_im)], axis=1).astype(BF16)
    vec = lambda v: v.reshape(S5_BLOCKS, 1, S5_HALF)
    return bblk, cblk, vec(ab_re), vec(ab_im), d_skip.reshape(S5_BLOCKS, 1, S5_CH)


def _s5(h, tables, s0_re, s0_im, batch, slen):
    bblk, cblk, are, aim, dsk = tables
    seg = S5_SEG if slen >= SUBLANES * S5_SEG else slen // SUBLANES
    apre, apim = are, aim
    for _ in range(int(math.log2(seg))):
        apre, apim = apre * apre - apim * apim, 2.0 * apre * apim
    tc = SUBLANES * seg
    nchunk = slen // tc
    pitch = S5_PITCH
    vec = pl.BlockSpec((None, 1, S5_HALF), lambda b, g, c: (g, 0, 0))
    state = pl.BlockSpec((None, None, 1, S5_HALF), lambda b, g, c: (b, g, 0, 0))
    y, sre, sim = pl.pallas_call(
        functools.partial(_s5_kernel, seg=seg, pitch=pitch),
        grid=(batch, S5_BLOCKS, nchunk),
        in_specs=[
            pl.BlockSpec((tc, S5_CH), lambda b, g, c: (b * nchunk + c, g)),
            pl.BlockSpec((None, S5_CH, 2 * S5_HALF), lambda b, g, c: (g, 0, 0)),
            pl.BlockSpec((None, 2 * S5_HALF, S5_CH), lambda b, g, c: (g, 0, 0)),
            vec, vec, vec, vec,
            pl.BlockSpec((None, 1, S5_CH), lambda b, g, c: (g, 0, 0)),
            state, state,
        ],
        out_specs=[
            pl.BlockSpec((tc, S5_CH), lambda b, g, c: (b * nchunk + c, g)),
            state, state,
        ],
        out_shape=[
            jax.ShapeDtypeStruct((batch * slen, MIX_WIDTH), F32),
            jax.ShapeDtypeStruct((batch, S5_BLOCKS, 1, S5_HALF), F32),
            jax.ShapeDtypeStruct((batch, S5_BLOCKS, 1, S5_HALF), F32),
        ],
        scratch_shapes=[
            pltpu.VMEM((S5_CH // LANES, SUBLANES * pitch, LANES), F32),
            pltpu.VMEM((tc, S5_CH), F32),
            pltpu.VMEM((tc, 2 * S5_HALF), F32),
            pltpu.VMEM((S5_CH // LANES, tc, LANES), F32),
            pltpu.VMEM((1, S5_HALF), F32),
            pltpu.VMEM((1, S5_HALF), F32),
        ],
        compiler_params=_params("parallel", "parallel", "arbitrary"),
        name="s5_mixer",
    )(h, bblk, cblk, are, aim, apre, apim, dsk, s0_re, s0_im)
    return y, sre, sim


def _diff_lambda(lq1, lk1, lq2, lk2, lam_init):
    e1 = jnp.exp(jnp.sum(lq1[...] * lk1[...], axis=-1, keepdims=True))
    e2 = jnp.exp(jnp.sum(lq2[...] * lk2[...], axis=-1, keepdims=True))
    return e1 - e2 + lam_init


def _softmax_update(s, v, m, l, acc):
    m_new = jnp.maximum(m, jnp.max(s, axis=-1, keepdims=True))
    alpha = jnp.exp(m - m_new)
    p = jnp.exp(s - m_new)
    l_new = alpha * l + jnp.sum(p, axis=-1, keepdims=True)
    acc_new = alpha * acc + _dot(p.astype(BF16), v)
    return m_new, l_new, acc_new


def _head_norm(o, g, lam_init):
    o = o * lax.rsqrt(jnp.mean(o * o, axis=-1, keepdims=True) + LN_EPS) * g
    return o * (1.0 - lam_init)


def _dattn_prompt_kernel(lq1, lk1, lq2, lk2, g_ref, q_ref, k_ref, v_ref, o_ref, k1s, k2s, vs, *, tq, lam_init):
    qi = pl.program_id(2)
    d = DIFF_HEAD_DIM
    scale = d ** -0.5

    @pl.when(qi == 0)
    def _():
        k1s[...] = k_ref[:, :d].astype(BF16)
        k2s[...] = k_ref[:, d:].astype(BF16)
        vs[...] = v_ref[...].astype(BF16)

    lam = _diff_lambda(lq1, lk1, lq2, lk2, lam_init)
    q1 = q_ref[:, :d].astype(BF16)
    q2 = q_ref[:, d:].astype(BF16)

    def block(j):
        rows = pl.ds(pl.multiple_of(j * tq, tq), tq)
        s1 = _dot_nt(q1, k1s[rows, :]) * scale
        s2 = _dot_nt(q2, k2s[rows, :]) * scale
        return s1, s2, vs[rows, :]

    def body(j, st):
        s1, s2, v = block(j)
        return _softmax_update(s1, v, *st[:3]) + _softmax_update(s2, v, *st[3:])

    init = (jnp.full((tq, 1), -jnp.inf, F32), jnp.zeros((tq, 1), F32), jnp.zeros((tq, DIFF_V_DIM), F32))
    st = lax.fori_loop(0, qi, body, init + init)

    s1, s2, v = block(qi)
    causal = lax.broadcasted_iota(jnp.int32, (tq, tq), 0) >= lax.broadcasted_iota(jnp.int32, (tq, tq), 1)
    s1 = jnp.where(causal, s1, -jnp.inf)
    s2 = jnp.where(causal, s2, -jnp.inf)
    _, l1, a1 = _softmax_update(s1, v, *st[:3])
    _, l2, a2 = _softmax_update(s2, v, *st[3:])
    o = a1 / l1 - lam * (a2 / l2)
    o_ref[...] = _head_norm(o, g_ref[...], lam_init).astype(o_ref.dtype)


def _row3(v):
    return v.reshape(v.shape[0], 1, v.shape[1])


def _dattn_prompt(h, w, layer, lam_init, batch, slen, *, tq):
    nq = slen // tq
    lam_spec = pl.BlockSpec((None, 1, DIFF_HEAD_DIM), lambda b, hh, qi: (layer, 0, 0))
    row3 = _row3
    return pl.pallas_call(
        functools.partial(_dattn_prompt_kernel, tq=tq, lam_init=lam_init),
        grid=(batch, DIFF_HEADS, nq),
        in_specs=[
            lam_spec, lam_spec, lam_spec, lam_spec,
            pl.BlockSpec((None, 1, DIFF_V_DIM), lambda b, hh, qi: (layer, 0, 0)),
            pl.BlockSpec((tq, DIFF_V_DIM), lambda b, hh, qi: (b * nq + qi, hh)),
            pl.BlockSpec((slen, DIFF_V_DIM), lambda b, hh, qi: (b, DIFF_HEADS + hh)),
            pl.BlockSpec((slen, DIFF_V_DIM), lambda b, hh, qi: (b, 2 * DIFF_HEADS + hh)),
        ],
        out_specs=pl.BlockSpec((tq, DIFF_V_DIM), lambda b, hh, qi: (b * nq + qi, hh)),
        out_shape=jax.ShapeDtypeStruct((batch * slen, MIX_WIDTH), BF16),
        scratch_shapes=[
            pltpu.VMEM((slen, DIFF_HEAD_DIM), BF16),
            pltpu.VMEM((slen, DIFF_HEAD_DIM), BF16),
            pltpu.VMEM((slen, DIFF_V_DIM), BF16),
        ],
        compiler_params=_params("parallel", "parallel", "arbitrary"),
        name="diff_attn_prompt",
    )(row3(w['diff_lambda_q1']), row3(w['diff_lambda_k1']), row3(w['diff_lambda_q2']), row3(w['diff_lambda_k2']),
      row3(w['diff_subln_g']), h, h, h)


def _dattn_decode_kernel(pt_ref, lq1, lk1, lq2, lk2, g_ref, q_ref, kn_ref, vn_ref, *rest, tnew, lam_init):
    del pt_ref
    kc_refs = rest[:DECODE_PAGES]
    vc_refs = rest[DECODE_PAGES:2 * DECODE_PAGES]
    o_ref, m_s, l_s, acc_s, q_s = rest[2 * DECODE_PAGES:]
    p = pl.program_id(1)
    d = DIFF_HEAD_DIM
    scale = d ** -0.5
    kv_rows = DIFF_HEADS * 2
    hrows = 2 * tnew

    @pl.when(p == 0)
    def _():
        m_s[...] = jnp.full_like(m_s, -jnp.inf)
        l_s[...] = jnp.zeros_like(l_s)
        acc_s[...] = jnp.zeros_like(acc_s)
        q_s[...] = q_ref[...].astype(BF16)

    def q_of(j):
        return q_s[:, j * d:(j + 1) * d]

    def update(s, pv_of):
        m_old = m_s[...]
        m_new = jnp.maximum(m_old, jnp.max(s, axis=-1, keepdims=True))
        alpha = jnp.exp(m_old - m_new)
        prob = jnp.exp(s - m_new)
        l_s[...] = alpha * l_s[...] + jnp.sum(prob, axis=-1, keepdims=True)
        m_s[...] = m_new
        pb = prob.astype(BF16)
        pv = jnp.concatenate([pv_of(hh, pb[hh * hrows:(hh + 1) * hrows]) for hh in range(DIFF_HEADS)], axis=0)
        acc_s[...] = alpha * acc_s[...] + pv

    s_pages = []
    for kc_ref in kc_refs:
        s_pages.append(jnp.concatenate(
            [_dot_nt(q_of(j), kc_ref[pl.ds(j, PAGE_SIZE, stride=kv_rows), :].astype(BF16)) for j in range(kv_rows)],
            axis=0))
    s_all = jnp.concatenate(s_pages, axis=1) * scale

    def pv_pages(hh, pb):
        out = None
        for g, vc_ref in enumerate(vc_refs):
            term = _dot(pb[:, g * PAGE_SIZE:(g + 1) * PAGE_SIZE], vc_ref[hh].astype(BF16))
            out = term if out is None else out + term
        return out

    update(s_all, pv_pages)

    @pl.when(p == pl.num_programs(1) - 1)
    def _():
        lam = _diff_lambda(lq1, lk1, lq2, lk2, lam_init)
        pad_rows = jnp.zeros((LANES - tnew, d), BF16)
        pad_v = jnp.zeros((LANES - tnew, DIFF_V_DIM), BF16)
        row = lax.broadcasted_iota(jnp.int32, (tnew, LANES), 0)
        col = lax.broadcasted_iota(jnp.int32, (tnew, LANES), 1)
        causal = jnp.concatenate([col <= row] * kv_rows, axis=0)
        s_new = jnp.concatenate(
            [_dot_nt(q_of(j), jnp.concatenate([kn_ref[:, j * d:(j + 1) * d].astype(BF16), pad_rows], axis=0))
             for j in range(kv_rows)], axis=0) * scale

        def pv_new(hh, pb):
            v = jnp.concatenate([vn_ref[:, hh * DIFF_V_DIM:(hh + 1) * DIFF_V_DIM].astype(BF16), pad_v], axis=0)
            return _dot(pb, v)

        update(jnp.where(causal, s_new, -jnp.inf), pv_new)
        acc = acc_s[...] / l_s[...]
        for hh in range(DIFF_HEADS):
            o = acc[hh * hrows:hh * hrows + tnew] - lam * acc[hh * hrows + tnew:(hh + 1) * hrows]
            o_ref[:, hh * DIFF_V_DIM:(hh + 1) * DIFF_V_DIM] = _head_norm(o, g_ref[...], lam_init).astype(o_ref.dtype)


def _dattn_decode(h, k_pool, v_pool, page_table, w, layer, lam_init, batch, tnew):
    n_pool = k_pool.shape[1]
    n_pages = page_table.shape[1]
    kv_rows = DIFF_HEADS * 2
    kc = k_pool[layer].reshape(n_pool * PAGE_SIZE * kv_rows, DIFF_HEAD_DIM)
    vc = jnp.transpose(v_pool[layer], (0, 2, 1, 3))
    lam_spec = pl.BlockSpec((None, 1, DIFF_HEAD_DIM), lambda b, p, pt: (layer, 0, 0))
    row3 = _row3
    new_spec = lambda blk: pl.BlockSpec((tnew, MIX_WIDTH), lambda b, p, pt: (b, blk))
    k_spec = lambda g: pl.BlockSpec((PAGE_SIZE * kv_rows, DIFF_HEAD_DIM),
                                    lambda b, p, pt: (pt[b, p * DECODE_PAGES + g], 0))
    v_spec = lambda g: pl.BlockSpec((None, DIFF_HEADS, PAGE_SIZE, DIFF_V_DIM),
                                    lambda b, p, pt: (pt[b, p * DECODE_PAGES + g], 0, 0, 0))
    state_rows = kv_rows * tnew
    grid_spec = pltpu.PrefetchScalarGridSpec(
        num_scalar_prefetch=1,
        grid=(batch, n_pages // DECODE_PAGES),
        in_specs=[
            lam_spec, lam_spec, lam_spec, lam_spec,
            pl.BlockSpec((None, 1, DIFF_V_DIM), lambda b, p, pt: (layer, 0, 0)),
            new_spec(0), new_spec(1), new_spec(2),
            *[k_spec(g) for g in range(DECODE_PAGES)],
            *[v_spec(g) for g in range(DECODE_PAGES)],
        ],
        out_specs=pl.BlockSpec((tnew, MIX_WIDTH), lambda b, p, pt: (b, 0)),
        scratch_shapes=[
            pltpu.VMEM((state_rows, 1), F32),
            pltpu.VMEM((state_rows, 1), F32),
            pltpu.VMEM((state_rows, DIFF_V_DIM), F32),
            pltpu.VMEM((tnew, MIX_WIDTH), BF16),
        ],
    )
    return pl.pallas_call(
        functools.partial(_dattn_decode_kernel, tnew=tnew, lam_init=lam_init),
        grid_spec=grid_spec,
        out_shape=jax.ShapeDtypeStruct((batch * tnew, MIX_WIDTH), _act_dtype(tnew)),
        compiler_params=_params("parallel", "arbitrary"),
        name="diff_attn_decode",
    )(page_table, row3(w['diff_lambda_q1']), row3(w['diff_lambda_k1']), row3(w['diff_lambda_q2']),
      row3(w['diff_lambda_k2']), row3(w['diff_subln_g']), h, h, h, *([kc] * DECODE_PAGES), *([vc] * DECODE_PAGES))


def _mem_attn_kernel(q_ref, mk_ref, mv_ref, o_ref):
    s = _dot_nt(q_ref[...].astype(BF16), mk_ref[...].astype(BF16)) * (MEM_HEAD_DIM ** -0.5)
    p = jnp.exp(s - jnp.max(s, axis=-1, keepdims=True))
    l = jnp.sum(p, axis=-1, keepdims=True)
    o_ref[...] = (_dot(p.astype(BF16), mv_ref[...].astype(BF16)) / l).astype(o_ref.dtype)


def _mem_attn(h, q_col, mem_kv, batch, slen, *, ts):
    ns = slen // ts
    return pl.pallas_call(
        _mem_attn_kernel,
        grid=(batch, ns, MEM_HEADS),
        in_specs=[
            pl.BlockSpec((ts, MEM_HEAD_DIM), lambda b, si, hh: (b * ns + si, q_col + hh)),
            pl.BlockSpec((MEM_TOKENS, MEM_HEAD_DIM), lambda b, si, hh: (b, hh)),
            pl.BlockSpec((MEM_TOKENS, MEM_HEAD_DIM), lambda b, si, hh: (b, MEM_HEADS + hh)),
        ],
        out_specs=pl.BlockSpec((ts, MEM_HEAD_DIM), lambda b, si, hh: (b * ns + si, hh)),
        out_shape=jax.ShapeDtypeStruct((batch * slen, MEM_WIDTH), _act_dtype(ts)),
        compiler_params=_params("parallel", "parallel", "parallel"),
        name="mem_attn",
    )(h, mem_kv, mem_kv)


def _tiles(m, streaming):
    if streaming:
        return dict(mm=dict(tm=m, tn=1024, tk=2048), mm_bf=dict(tm=m, tn=1024, tk=2048),
                    glu=dict(tm=m, tn=1024, tk=1024), ln=dict(tm=m, tn=256), ffn=dict(tm=m, tf=256))
    return dict(mm=dict(tm=2048, tn=1024, tk=1024), mm_bf=dict(tm=2048, tn=1024, tk=2048),
                glu=dict(tm=1024, tn=1024, tk=1024), ln=dict(tm=512, tn=256), ffn=dict(tm=1024, tf=256))


def _run_trunk(x, batch, slen, mem_kv, ssm_state, attn_past, w, wb):
    m = batch * slen
    emit = wb is None
    t = _tiles(m, emit)
    ts = min(slen, 512)
    new_k = new_v = s_re = s_im = None
    copies = {}

    def in_proj(x_in, tensor, li, i):
        tiles = t['mm_bf'] if x_in.dtype == BF16 else t['mm']
        if emit:
            h, copies['in', i] = _matmul(x_in, tensor, li, emit=True, **tiles)
            return h
        return _matmul(x_in, wb['in', i], **tiles)

    def proj_ln(name, a, tensor, x_res, g, b, i):
        if emit:
            xf, xb, copies[name, i] = _matmul_ln(a, tensor, x_res, g, b, i, emit=True, **t['ln'])
            return xf, xb
        return _matmul_ln(a, wb[name, i], x_res, g, b, i, **t['ln'])

    x_f32, x_in = x, x
    for i in range(DEPTH):
        li = i // 2
        if i % 2 == 0:
            h = in_proj(x_in, w['w_in_ssm'], li, i)
            if ssm_state is None:
                s0_re = s0_im = jnp.zeros((batch, S5_BLOCKS, 1, S5_HALF), F32)
            else:
                s0_re = ssm_state[0][li].reshape(batch, S5_BLOCKS, 1, S5_HALF)
                s0_im = ssm_state[1][li].reshape(batch, S5_BLOCKS, 1, S5_HALF)
            y, s_re, s_im = _s5(h, w['s5_tables'][li], s0_re, s0_im, batch, slen)
            if emit:
                mix, copies['glu', i] = _glu(y, w['w_glu'], w['b_glu'], li, _act_dtype(ts), emit=True, **t['glu'])
            else:
                mix = _glu(y, wb['glu', i], w['b_glu'], li, _act_dtype(ts), **t['glu'])
            q_col = MIX_WIDTH // MEM_HEAD_DIM
        else:
            h = in_proj(x_in, w['w_in_attn'], li, i)
            lam_init = 0.8 - 0.6 * math.exp(-0.3 * i)
            if attn_past is None:
                mix = _dattn_prompt(h, w, li, lam_init, batch, slen, tq=PROMPT_Q_BLOCK)
            else:
                mix = _dattn_decode(h, attn_past[0], attn_past[1], attn_past[2], w, li, lam_init, batch, slen)
            new_k = h[:, MIX_WIDTH:2 * MIX_WIDTH]
            new_v = h[:, 2 * MIX_WIDTH:3 * MIX_WIDTH]
            q_col = 3 * MIX_WIDTH // MEM_HEAD_DIM
        mem = _mem_attn(h, q_col, mem_kv[i], batch, slen, ts=ts)
        cat = jnp.concatenate([mix, mem], axis=-1)
        x_f32, x_bf = proj_ln('o', cat, w['w_o'], x_f32, w['ln1_g'], w['ln1_b'], i)
        if emit:
            hid, copies['ffn_in', i] = _ffn_in(x_bf, w['w_ffn_in'], i, emit=True, **t['ffn'])
        else:
            hid = _ffn_in(x_bf, wb['ffn_in', i], **t['ffn'])
        x_f32, x_in = proj_ln('ffn_out', hid, w['w_ffn_out'], x_f32, w['ln2_g'], w['ln2_b'], i)
    return (x_f32, new_k, new_v, s_re, s_im), copies


def kernel(x_prompt, x_sample, cache_attn_k, cache_attn_v, state_ssm_re, state_ssm_im, cache_mem_k, cache_mem_v,
           page_table, mem_prompt, w_in_ssm, ssm_lambda_re, ssm_lambda_im, ssm_log_dt, ssm_b_re, ssm_b_im,
           ssm_c_re, ssm_c_im, ssm_d, w_glu, b_glu, w_in_attn, diff_lambda_q1, diff_lambda_k1, diff_lambda_q2,
           diff_lambda_k2, diff_subln_g, w_mem_kv, w_o, ln1_g, ln1_b, w_ffn_in, w_ffn_out, ln2_g, ln2_b):
    w = dict(w_in_ssm=w_in_ssm, ssm_lambda_re=ssm_lambda_re, ssm_lambda_im=ssm_lambda_im, ssm_log_dt=ssm_log_dt,
             ssm_b_re=ssm_b_re, ssm_b_im=ssm_b_im, ssm_c_re=ssm_c_re, ssm_c_im=ssm_c_im, ssm_d=ssm_d,
             w_glu=w_glu, b_glu=b_glu, w_in_attn=w_in_attn, diff_lambda_q1=diff_lambda_q1,
             diff_lambda_k1=diff_lambda_k1, diff_lambda_q2=diff_lambda_q2, diff_lambda_k2=diff_lambda_k2,
             diff_subln_g=diff_subln_g, w_o=w_o, ln1_g=ln1_g, ln1_b=ln1_b, w_ffn_in=w_ffn_in,
             w_ffn_out=w_ffn_out, ln2_g=ln2_g, ln2_b=ln2_b)
    bp, sp, _ = x_prompt.shape
    bs, ss, _ = x_sample.shape

    mem_rows = mem_prompt.reshape(bp * MEM_TOKENS, D_MODEL)
    mem_kv_p = [_matmul(mem_rows, w_mem_kv, l, tm=bp * MEM_TOKENS, tn=1024, tk=1024) for l in range(DEPTH)]
    mem_k_prompt = jnp.stack([kv[:, :MEM_WIDTH].reshape(bp, MEM_TOKENS, MEM_HEADS, MEM_HEAD_DIM) for kv in mem_kv_p])
    mem_v_prompt = jnp.stack([kv[:, MEM_WIDTH:].reshape(bp, MEM_TOKENS, MEM_HEADS, MEM_HEAD_DIM) for kv in mem_kv_p])
    mem_kv_s = [jnp.concatenate([cache_mem_k[l].reshape(bs * MEM_TOKENS, MEM_WIDTH),
                                 cache_mem_v[l].reshape(bs * MEM_TOKENS, MEM_WIDTH)], axis=1) for l in range(DEPTH)]

    w['s5_tables'] = [_s5_tables(ssm_lambda_re[l], ssm_lambda_im[l], ssm_log_dt[l], ssm_b_re[l], ssm_b_im[l],
                                 ssm_c_re[l], ssm_c_im[l], ssm_d[l]) for l in range(ssm_d.shape[0])]

    (ys, ks, vs, srs, sis), wb = _run_trunk(x_sample.reshape(bs * ss, D_MODEL), bs, ss, mem_kv_s,
                                            (state_ssm_re, state_ssm_im),
                                            (cache_attn_k, cache_attn_v, page_table), w, None)
    (yp, kp, vp, srp, sip), _ = _run_trunk(x_prompt.reshape(bp * sp, D_MODEL), bp, sp, mem_kv_p, None, None, w, wb)

    def kshape(k, b, s):
        return k.reshape(1, b, s, DIFF_HEADS, 2, DIFF_HEAD_DIM)

    def vshape(v, b, s):
        return v.reshape(1, b, s, DIFF_HEADS, DIFF_V_DIM)

    def sshape(s, b):
        return s.reshape(1, b, SSM_GROUPS, SSM_STATE)

    return (yp.reshape(bp, sp, D_MODEL), ys.reshape(bs, ss, D_MODEL),
            kshape(kp, bp, sp), vshape(vp, bp, sp), kshape(ks, bs, ss), vshape(vs, bs, ss),
            sshape(srp, bp), sshape(sip, bp), sshape(srs, bs), sshape(sis, bs),
            mem_k_prompt, mem_v_prompt)
```

```python
import functools
import math

import jax
import jax.numpy as jnp
from jax import lax
from jax.experimental import pallas as pl
from jax.experimental.pallas import tpu as pltpu

F32 = jnp.float32
BF16 = jnp.bfloat16

D_MODEL = 4096
DEPTH = 2
MEM_TOKENS = 256
MEM_HEADS = 4
MEM_HEAD_DIM = 256
MEM_WIDTH = MEM_HEADS * MEM_HEAD_DIM
MIX_WIDTH = D_MODEL - MEM_WIDTH
SSM_GROUP = 16
SSM_GROUPS = MIX_WIDTH // SSM_GROUP
SSM_STATE = 64
DIFF_HEAD_DIM = 128
DIFF_HEADS = MIX_WIDTH // (2 * DIFF_HEAD_DIM)
DIFF_V_DIM = 2 * DIFF_HEAD_DIM
D_FF = 11008
ALPHA = (2 * DEPTH) ** 0.25
LN_EPS = 1e-5
PAGE_SIZE = 128

VMEM_LIMIT_BYTES = 58 * 1024 * 1024
LANES = 128
SUBLANES = 8

S5_GROUPS_PER_BLOCK = 16
S5_CH = S5_GROUPS_PER_BLOCK * SSM_GROUP
S5_HALF = S5_GROUPS_PER_BLOCK * SSM_STATE
S5_BLOCKS = SSM_GROUPS // S5_GROUPS_PER_BLOCK
S5_SEG = 64
S5_PITCH = S5_SEG + SUBLANES
LN_ROWS = 128
DECODE_PAGES = 4
PROMPT_Q_BLOCK = 512


def _params(*sem):
    return pltpu.CompilerParams(dimension_semantics=sem, vmem_limit_bytes=VMEM_LIMIT_BYTES)


def _dot(a, b):
    return jnp.dot(a, b, preferred_element_type=F32)


def _dot_nt(a, b):
    return lax.dot_general(a, b, (((1,), (1,)), ((), ())), preferred_element_type=F32)


def _w_spec(w, layer, block, index):
    if w.ndim == 3:
        return pl.BlockSpec((None,) + block, lambda *g: (layer,) + index(*g))
    return pl.BlockSpec(block, index)


def _accumulate(ref, first, term):
    @pl.when(first)
    def _():
        ref[...] = term()

    @pl.when(jnp.logical_not(first))
    def _():
        ref[...] += term()


def _mm_kernel(x_ref, w_ref, o_ref, *wb_ref):
    w = w_ref[...].astype(BF16)
    if wb_ref:
        wb_ref[0][...] = w
    _accumulate(o_ref, pl.program_id(2) == 0, lambda: _dot(x_ref[...].astype(BF16), w))


def _matmul(x, w, layer=None, *, tm, tn, tk, emit=False):
    m, k = x.shape
    n = w.shape[-1]
    assert not emit or m == tm
    out_specs = [pl.BlockSpec((tm, tn), lambda i, j, kk: (i, j))]
    out_shape = [jax.ShapeDtypeStruct((m, n), F32)]
    if emit:
        out_specs.append(pl.BlockSpec((tk, tn), lambda i, j, kk: (kk, j)))
        out_shape.append(jax.ShapeDtypeStruct((k, n), BF16))
    res = pl.pallas_call(
        _mm_kernel,
        grid=(m // tm, n // tn, k // tk),
        in_specs=[
            pl.BlockSpec((tm, tk), lambda i, j, kk: (i, kk)),
            _w_spec(w, layer, (tk, tn), lambda i, j, kk: (kk, j)),
        ],
        out_specs=out_specs,
        out_shape=out_shape,
        compiler_params=_params("parallel", "parallel", "arbitrary"),
        name="matmul",
    )(x, w)
    return res if emit else res[0]


def _glu_kernel(yk_ref, w_ref, b_ref, yn_ref, o_ref, *rest):
    acc_ref = rest[-1]
    kk = pl.program_id(2)
    w = w_ref[...].astype(BF16)
    if len(rest) == 2:
        rest[0][...] = w
    _accumulate(acc_ref, kk == 0, lambda: _dot(yk_ref[...].astype(BF16), w))

    @pl.when(kk == pl.num_programs(2) - 1)
    def _():
        gate = jax.nn.sigmoid(acc_ref[...] + b_ref[...])
        o_ref[...] = (yn_ref[...] * gate).astype(o_ref.dtype)


def _act_dtype(rows_per_block):
    return BF16 if rows_per_block % (2 * SUBLANES) == 0 else F32


def _glu(y, w, b, layer, out_dtype, *, tm, tn, tk, emit=False):
    m, k = y.shape
    n = w.shape[-1]
    assert not emit or m == tm
    out_specs = [pl.BlockSpec((tm, tn), lambda i, j, kk: (i, j))]
    out_shape = [jax.ShapeDtypeStruct((m, n), out_dtype)]
    if emit:
        out_specs.append(pl.BlockSpec((tk, tn), lambda i, j, kk: (kk, j)))
        out_shape.append(jax.ShapeDtypeStruct((k, n), BF16))
    res = pl.pallas_call(
        _glu_kernel,
        grid=(m // tm, n // tn, k // tk),
        in_specs=[
            pl.BlockSpec((tm, tk), lambda i, j, kk: (i, kk)),
            _w_spec(w, layer, (tk, tn), lambda i, j, kk: (kk, j)),
            pl.BlockSpec((None, 1, tn), lambda i, j, kk: (layer, 0, j)),
            pl.BlockSpec((tm, tn), lambda i, j, kk: (i, j)),
        ],
        out_specs=out_specs,
        out_shape=out_shape,
        scratch_shapes=[pltpu.VMEM((tm, tn), F32)],
        compiler_params=_params("parallel", "parallel", "arbitrary"),
        name="glu",
    )(y, w, b.reshape(b.shape[0], 1, b.shape[1]), y)
    return res if emit else res[0]


def _mm_ln_kernel(*refs, n_a):
    a_refs = refs[:n_a]
    w_ref, x_ref, g_ref, b_ref, of_ref, ob_ref, *rest = refs[n_a:]
    acc = rest[-1]
    j = pl.program_id(1)
    w = w_ref[...].astype(BF16)
    if len(rest) == 2:
        rest[0][...] = w
    total, k0 = None, 0
    for a_ref in a_refs:
        k1 = k0 + a_ref.shape[1]
        term = _dot(a_ref[...].astype(BF16), w[k0:k1])
        total = term if total is None else total + term
        k0 = k1
    acc[j] = total

    @pl.when(j == pl.num_programs(1) - 1)
    def _():
        panels, tm, tn = acc.shape
        rc = min(tm, LN_ROWS)
        width = panels * tn

        def rows(c, _):
            r = pl.ds(pl.multiple_of(c * rc, rc), rc)
            cols = [slice(p * tn, (p + 1) * tn) for p in range(panels)]
            ys = [ALPHA * x_ref[r, cs] + acc[p, r, :] for p, cs in enumerate(cols)]
            mu = sum(jnp.sum(y, axis=-1, keepdims=True) for y in ys) / width
            ds = [y - mu for y in ys]
            var = sum(jnp.sum(d * d, axis=-1, keepdims=True) for d in ds) / width
            inv = lax.rsqrt(var + LN_EPS)
            for d, cs in zip(ds, cols):
                out = d * inv * g_ref[:, cs] + b_ref[:, cs]
                of_ref[r, cs] = out
                ob_ref[r, cs] = out.astype(BF16)
            return 0

        lax.fori_loop(0, tm // rc, rows, 0)


def _panel_spec(w, layer, k, tn):
    if w.dtype == BF16:
        assert w.shape[1:] == (k, tn)
        return pl.BlockSpec((None, k, tn), lambda i, j: (j, 0, 0))
    return pl.BlockSpec((None, k, tn), lambda i, j: (layer, 0, j))


def _matmul_ln(a_parts, w, x, g, b, layer, *, tm, tn, emit=False):
    m, n = x.shape
    k = sum(a.shape[1] for a in a_parts)
    assert not emit or m == tm
    once = pl.Buffered(1)
    out_specs = [
        pl.BlockSpec((tm, n), lambda i, j: (i, 0), pipeline_mode=once),
        pl.BlockSpec((tm, n), lambda i, j: (i, 0), pipeline_mode=once),
    ]
    out_shape = [jax.ShapeDtypeStruct((m, n), F32), jax.ShapeDtypeStruct((m, n), BF16)]
    if emit:
        out_specs.append(pl.BlockSpec((None, k, tn), lambda i, j: (j, 0, 0)))
        out_shape.append(jax.ShapeDtypeStruct((n // tn, k, tn), BF16))
    return pl.pallas_call(
        functools.partial(_mm_ln_kernel, n_a=len(a_parts)),
        grid=(m // tm, n // tn),
        in_specs=[
            *[pl.BlockSpec((tm, a.shape[1]), lambda i, j: (i, 0), pipeline_mode=once) for a in a_parts],
            _panel_spec(w, layer, k, tn),
            pl.BlockSpec((tm, n), lambda i, j: (i, 0), pipeline_mode=once),
            pl.BlockSpec((None, 1, n), lambda i, j: (layer, 0, 0)),
            pl.BlockSpec((None, 1, n), lambda i, j: (layer, 0, 0)),
        ],
        out_specs=out_specs,
        out_shape=out_shape,
        scratch_shapes=[pltpu.VMEM((n // tn, tm, tn), F32)],
        compiler_params=_params("parallel", "arbitrary"),
        name="matmul_ln",
    )(*a_parts, w, x, g.reshape(g.shape[0], 1, n), b.reshape(b.shape[0], 1, n))


def _ffn_in_kernel(x_ref, wg_ref, wu_ref, o_ref, *wb_refs):
    x = x_ref[...]
    wg = wg_ref[...].astype(BF16)
    wu = wu_ref[...].astype(BF16)
    if wb_refs:
        wb_refs[0][...] = wg
        wb_refs[1][...] = wu
    o_ref[...] = (jax.nn.silu(_dot(x, wg)) * _dot(x, wu)).astype(o_ref.dtype)


def _ffn_in(x, w, layer=None, *, tm, tf, emit=False):
    m, k = x.shape
    nf = D_FF // tf
    assert not emit or m == tm
    if isinstance(w, tuple):
        wg, wu = w
        w_specs = [_panel_spec(wg, None, k, tf), _panel_spec(wu, None, k, tf)]
    else:
        wg = wu = w
        w_specs = [pl.BlockSpec((None, k, tf), lambda i, j: (layer, 0, j)),
                   pl.BlockSpec((None, k, tf), lambda i, j: (layer, 0, j + nf))]
    out_specs = [pl.BlockSpec((tm, tf), lambda i, j: (i, j))]
    out_shape = [jax.ShapeDtypeStruct((m, D_FF), BF16)]
    if emit:
        out_specs += [pl.BlockSpec((None, k, tf), lambda i, j: (j, 0, 0))] * 2
        out_shape += [jax.ShapeDtypeStruct((nf, k, tf), BF16)] * 2
    res = pl.pallas_call(
        _ffn_in_kernel,
        grid=(m // tm, nf),
        in_specs=[pl.BlockSpec((tm, k), lambda i, j: (i, 0), pipeline_mode=pl.Buffered(1))] + w_specs,
        out_specs=out_specs,
        out_shape=out_shape,
        compiler_params=_params("parallel", "arbitrary"),
        name="ffn_in",
    )(x, wg, wu)
    return (res[0], (res[1], res[2])) if emit else res[0]


def _s5_kernel(u_ref, bb_ref, cb_ref, are_ref, aim_ref, apre_ref, apim_ref, d_ref, s0re_ref, s0im_ref,
               y_ref, sre_ref, sim_ref,
               uslab, uperm, bu, yslab, cre, cim, *, seg, pitch):
    half = S5_HALF

    @pl.when(pl.program_id(2) == 0)
    def _():
        cre[...] = s0re_ref[...]
        cim[...] = s0im_ref[...]

    if seg == 1:
        uperm[...] = u_ref[...]
    else:
        for i in range(SUBLANES):
            for j in range(S5_CH // LANES):
                uslab[j, i * pitch:i * pitch + seg, :] = u_ref[i * seg:(i + 1) * seg, j * LANES:(j + 1) * LANES]

        def gather(r, _):
            row = pl.ds(pl.multiple_of(r * SUBLANES, SUBLANES), SUBLANES)
            for j in range(S5_CH // LANES):
                uperm[row, j * LANES:(j + 1) * LANES] = uslab[j, pl.ds(r, SUBLANES, stride=pitch), :]
            return 0

        lax.fori_loop(0, seg, gather, 0, unroll=min(seg, 4))

    up = uperm[...]
    bu[...] = _dot(up.astype(BF16), bb_ref[...])

    are = jnp.broadcast_to(are_ref[...], (SUBLANES, half))
    aim = jnp.broadcast_to(aim_ref[...], (SUBLANES, half))

    def advance(r, sre, sim):
        row = pl.ds(pl.multiple_of(r * SUBLANES, SUBLANES), SUBLANES)
        nre = are * sre - aim * sim + bu[row, :half]
        nim = are * sim + aim * sre + bu[row, half:]
        return row, nre, nim

    def local_step(r, carry):
        _, nre, nim = advance(r, *carry)
        return nre, nim

    zero = jnp.zeros((SUBLANES, half), F32)
    ere, eim = lax.fori_loop(0, seg, local_step, (zero, zero), unroll=min(seg, 2))

    apre = apre_ref[...]
    apim = apim_ref[...]
    rows_re = [cre[...]]
    rows_im = [cim[...]]
    for i in range(1, SUBLANES + 1):
        pre, pim = rows_re[-1], rows_im[-1]
        rows_re.append(ere[i - 1:i] + apre * pre - apim * pim)
        rows_im.append(eim[i - 1:i] + apre * pim + apim * pre)
    cre[...] = rows_re[-1]
    cim[...] = rows_im[-1]
    sre_ref[...] = rows_re[-1]
    sim_ref[...] = rows_im[-1]
    cin_re = jnp.concatenate(rows_re[:SUBLANES], axis=0)
    cin_im = jnp.concatenate(rows_im[:SUBLANES], axis=0)

    def true_step(r, carry):
        row, nre, nim = advance(r, *carry)
        bu[row, :half] = nre
        bu[row, half:] = nim
        return nre, nim

    lax.fori_loop(0, seg, true_step, (cin_re, cin_im), unroll=min(seg, 2))

    yp = _dot(bu[...].astype(BF16), cb_ref[...]) + d_ref[...] * up
    yp = jax.nn.gelu(yp)
    if seg == 1:
        y_ref[...] = yp
    else:
        for j in range(S5_CH // LANES):
            yslab[j] = yp[:, j * LANES:(j + 1) * LANES]

        def scatter(q, _):
            for i in range(SUBLANES):
                dst = pl.ds(pl.multiple_of(i * seg + q * SUBLANES, SUBLANES), SUBLANES)
                src = pl.ds(q * SUBLANES * SUBLANES + i, SUBLANES, stride=SUBLANES)
                for j in range(S5_CH // LANES):
                    y_ref[dst, j * LANES:(j + 1) * LANES] = yslab[j, src, :]
            return 0

        lax.fori_loop(0, seg // SUBLANES, scatter, 0)


def _s5_tables(lam_re, lam_im, log_dt, b_re, b_im, c_re, c_im, d_skip):
    dt = jnp.exp(log_dt)[:, None]
    mag = jnp.exp(lam_re * dt)
    ab_re = mag * jnp.cos(lam_im * dt)
    ab_im = mag * jnp.sin(lam_im * dt)
    den = lam_re * lam_re + lam_im * lam_im
    nr = ab_re - 1.0
    f_re = (nr * lam_re + ab_im * lam_im) / den
    f_im = (ab_im * lam_re - nr * lam_im) / den
    bb_re = f_re[..., None] * b_re - f_im[..., None] * b_im
    bb_im = f_re[..., None] * b_im + f_im[..., None] * b_re
    eye = jnp.eye(S5_GROUPS_PER_BLOCK, dtype=F32)
    gpb = S5_GROUPS_PER_BLOCK

    def expand_b(bb):
        bb = bb.reshape(S5_BLOCKS, gpb, SSM_STATE, SSM_GROUP)
        return jnp.einsum('bgpc,gh->bgchp', bb, eye).reshape(S5_BLOCKS, S5_CH, S5_HALF)

    def expand_c(cc):
        cc = cc.reshape(S5_BLOCKS, gpb, SSM_GROUP, SSM_STATE)
        return jnp.einsum('bgcp,gh->bgphc', cc, eye).reshape(S5_BLOCKS, S5_HALF, S5_CH)

    bblk = jnp.concatenate([expand_b(bb_re), expand_b(bb_im)], axis=2).astype(BF16)
    cblk = jnp.concatenate([expand_c(c_re), -expand_c(c_im)], axis=1).astype(BF16)
    vec = lambda v: v.reshape(S5_BLOCKS, 1, S5_HALF)
    return bblk, cblk, vec(ab_re), vec(ab_im), d_skip.reshape(S5_BLOCKS, 1, S5_CH)


def _s5(h, tables, s0_re, s0_im, batch, slen):
    bblk, cblk, are, aim, dsk = tables
    seg = S5_SEG if slen >= SUBLANES * S5_SEG else slen // SUBLANES
    apre, apim = are, aim
    for _ in range(int(math.log2(seg))):
        apre, apim = apre * apre - apim * apim, 2.0 * apre * apim
    tc = SUBLANES * seg
    nchunk = slen // tc
    pitch = S5_PITCH
    vec = pl.BlockSpec((None, 1, S5_HALF), lambda b, g, c: (g, 0, 0))
    state = pl.BlockSpec((None, None, 1, S5_HALF), lambda b, g, c: (b, g, 0, 0))
    y, sre, sim = pl.pallas_call(
        functools.partial(_s5_kernel, seg=seg, pitch=pitch),
        grid=(batch, S5_BLOCKS, nchunk),
        in_specs=[
            pl.BlockSpec((tc, S5_CH), lambda b, g, c: (b * nchunk + c, g)),
            pl.BlockSpec((None, S5_CH, 2 * S5_HALF), lambda b, g, c: (g, 0, 0)),
            pl.BlockSpec((None, 2 * S5_HALF, S5_CH), lambda b, g, c: (g, 0, 0)),
            vec, vec, vec, vec,
            pl.BlockSpec((None, 1, S5_CH), lambda b, g, c: (g, 0, 0)),
            state, state,
        ],
        out_specs=[
            pl.BlockSpec((tc, S5_CH), lambda b, g, c: (b * nchunk + c, g)),
            state, state,
        ],
        out_shape=[
            jax.ShapeDtypeStruct((batch * slen, MIX_WIDTH), F32),
            jax.ShapeDtypeStruct((batch, S5_BLOCKS, 1, S5_HALF), F32),
            jax.ShapeDtypeStruct((batch, S5_BLOCKS, 1, S5_HALF), F32),
        ],
        scratch_shapes=[
            pltpu.VMEM((S5_CH // LANES, SUBLANES * pitch, LANES), F32),
            pltpu.VMEM((tc, S5_CH), F32),
            pltpu.VMEM((tc, 2 * S5_HALF), F32),
            pltpu.VMEM((S5_CH // LANES, tc, LANES), F32),
            pltpu.VMEM((1, S5_HALF), F32),
            pltpu.VMEM((1, S5_HALF), F32),
        ],
        compiler_params=_params("parallel", "parallel", "arbitrary"),
        name="s5_mixer",
    )(h, bblk, cblk, are, aim, apre, apim, dsk, s0_re, s0_im)
    return y, sre, sim


def _diff_lambda(lq1, lk1, lq2, lk2, lam_init):
    e1 = jnp.exp(jnp.sum(lq1[...] * lk1[...], axis=-1, keepdims=True))
    e2 = jnp.exp(jnp.sum(lq2[...] * lk2[...], axis=-1, keepdims=True))
    return e1 - e2 + lam_init


def _softmax_update(s, v, m, l, acc):
    m_new = jnp.maximum(m, jnp.max(s, axis=-1, keepdims=True))
    alpha = jnp.exp(m - m_new)
    p = jnp.exp(s - m_new)
    l_new = alpha * l + jnp.sum(p, axis=-1, keepdims=True)
    acc_new = alpha * acc + _dot(p.astype(BF16), v)
    return m_new, l_new, acc_new


def _head_norm(o, g, lam_init):
    o = o * lax.rsqrt(jnp.mean(o * o, axis=-1, keepdims=True) + LN_EPS) * g
    return o * (1.0 - lam_init)


def _dattn_prompt_kernel(lq1, lk1, lq2, lk2, g_ref, q_ref, k_ref, v_ref, o_ref, k1s, k2s, vs, *, tq, lam_init):
    qi = pl.program_id(2)
    d = DIFF_HEAD_DIM
    scale = d ** -0.5

    @pl.when(qi == 0)
    def _():
        k1s[...] = k_ref[:, :d].astype(BF16)
        k2s[...] = k_ref[:, d:].astype(BF16)
        vs[...] = v_ref[...].astype(BF16)

    lam = _diff_lambda(lq1, lk1, lq2, lk2, lam_init)
    q1 = q_ref[:, :d].astype(BF16)
    q2 = q_ref[:, d:].astype(BF16)

    def block(j):
        rows = pl.ds(pl.multiple_of(j * tq, tq), tq)
        s1 = _dot_nt(q1, k1s[rows, :]) * scale
        s2 = _dot_nt(q2, k2s[rows, :]) * scale
        return s1, s2, vs[rows, :]

    def body(j, st):
        s1, s2, v = block(j)
        return _softmax_update(s1, v, *st[:3]) + _softmax_update(s2, v, *st[3:])

    init = (jnp.full((tq, 1), -jnp.inf, F32), jnp.zeros((tq, 1), F32), jnp.zeros((tq, DIFF_V_DIM), F32))
    st = lax.fori_loop(0, qi, body, init + init)

    s1, s2, v = block(qi)
    causal = lax.broadcasted_iota(jnp.int32, (tq, tq), 0) >= lax.broadcasted_iota(jnp.int32, (tq, tq), 1)
    s1 = jnp.where(causal, s1, -jnp.inf)
    s2 = jnp.where(causal, s2, -jnp.inf)
    _, l1, a1 = _softmax_update(s1, v, *st[:3])
    _, l2, a2 = _softmax_update(s2, v, *st[3:])
    o = a1 / l1 - lam * (a2 / l2)
    o_ref[...] = _head_norm(o, g_ref[...], lam_init).astype(o_ref.dtype)


def _row3(v):
    return v.reshape(v.shape[0], 1, v.shape[1])


def _dattn_prompt(h, w, layer, lam_init, batch, slen, *, tq):
    nq = slen // tq
    lam_spec = pl.BlockSpec((None, 1, DIFF_HEAD_DIM), lambda b, hh, qi: (layer, 0, 0))
    row3 = _row3
    return pl.pallas_call(
        functools.partial(_dattn_prompt_kernel, tq=tq, lam_init=lam_init),
        grid=(batch, DIFF_HEADS, nq),
        in_specs=[
            lam_spec, lam_spec, lam_spec, lam_spec,
            pl.BlockSpec((None, 1, DIFF_V_DIM), lambda b, hh, qi: (layer, 0, 0)),
            pl.BlockSpec((tq, DIFF_V_DIM), lambda b, hh, qi: (b * nq + qi, hh)),
            pl.BlockSpec((slen, DIFF_V_DIM), lambda b, hh, qi: (b, DIFF_HEADS + hh)),
            pl.BlockSpec((slen, DIFF_V_DIM), lambda b, hh, qi: (b, 2 * DIFF_HEADS + hh)),
        ],
        out_specs=pl.BlockSpec((tq, DIFF_V_DIM), lambda b, hh, qi: (b * nq + qi, hh)),
        out_shape=jax.ShapeDtypeStruct((batch * slen, MIX_WIDTH), BF16),
        scratch_shapes=[
            pltpu.VMEM((slen, DIFF_HEAD_DIM), BF16),
            pltpu.VMEM((slen, DIFF_HEAD_DIM), BF16),
            pltpu.VMEM((slen, DIFF_V_DIM), BF16),
        ],
        compiler_params=_params("parallel", "parallel", "arbitrary"),
        name="diff_attn_prompt",
    )(row3(w['diff_lambda_q1']), row3(w['diff_lambda_k1']), row3(w['diff_lambda_q2']), row3(w['diff_lambda_k2']),
      row3(w['diff_subln_g']), h, h, h)


def _dattn_decode_kernel(pt_ref, lq1, lk1, lq2, lk2, g_ref, q_ref, kn_ref, vn_ref, *rest, tnew, lam_init):
    del pt_ref
    kc_refs = rest[:DECODE_PAGES]
    vc_refs = rest[DECODE_PAGES:2 * DECODE_PAGES]
    o_ref, m_s, l_s, acc_s, q_s = rest[2 * DECODE_PAGES:]
    p = pl.program_id(1)
    d = DIFF_HEAD_DIM
    scale = d ** -0.5
    kv_rows = DIFF_HEADS * 2
    hrows = 2 * tnew

    @pl.when(p == 0)
    def _():
        m_s[...] = jnp.full_like(m_s, -jnp.inf)
        l_s[...] = jnp.zeros_like(l_s)
        acc_s[...] = jnp.zeros_like(acc_s)
        q_s[...] = q_ref[...].astype(BF16)

    def q_of(j):
        return q_s[:, j * d:(j + 1) * d]

    def update(s, pv_of):
        m_old = m_s[...]
        m_new = jnp.maximum(m_old, jnp.max(s, axis=-1, keepdims=True))
        alpha = jnp.exp(m_old - m_new)
        prob = jnp.exp(s - m_new)
        l_s[...] = alpha * l_s[...] + jnp.sum(prob, axis=-1, keepdims=True)
        m_s[...] = m_new
        pb = prob.astype(BF16)
        pv = jnp.concatenate([pv_of(hh, pb[hh * hrows:(hh + 1) * hrows]) for hh in range(DIFF_HEADS)], axis=0)
        acc_s[...] = alpha * acc_s[...] + pv

    s_pages = []
    for kc_ref in kc_refs:
        s_pages.append(jnp.concatenate(
            [_dot_nt(q_of(j), kc_ref[pl.ds(j, PAGE_SIZE, stride=kv_rows), :].astype(BF16)) for j in range(kv_rows)],
            axis=0))
    s_all = jnp.concatenate(s_pages, axis=1) * scale

    def pv_pages(hh, pb):
        out = None
        for g, vc_ref in enumerate(vc_refs):
            term = _dot(pb[:, g * PAGE_SIZE:(g + 1) * PAGE_SIZE], vc_ref[hh].astype(BF16))
            out = term if out is None else out + term
        return out

    update(s_all, pv_pages)

    @pl.when(p == pl.num_programs(1) - 1)
    def _():
        lam = _diff_lambda(lq1, lk1, lq2, lk2, lam_init)
        pad_rows = jnp.zeros((LANES - tnew, d), BF16)
        pad_v = jnp.zeros((LANES - tnew, DIFF_V_DIM), BF16)
        row = lax.broadcasted_iota(jnp.int32, (tnew, LANES), 0)
        col = lax.broadcasted_iota(jnp.int32, (tnew, LANES), 1)
        causal = jnp.concatenate([col <= row] * kv_rows, axis=0)
        s_new = jnp.concatenate(
            [_dot_nt(q_of(j), jnp.concatenate([kn_ref[:, j * d:(j + 1) * d].astype(BF16), pad_rows], axis=0))
             for j in range(kv_rows)], axis=0) * scale

        def pv_new(hh, pb):
            v = jnp.concatenate([vn_ref[:, hh * DIFF_V_DIM:(hh + 1) * DIFF_V_DIM].astype(BF16), pad_v], axis=0)
            return _dot(pb, v)

        update(jnp.where(causal, s_new, -jnp.inf), pv_new)
        acc = acc_s[...] / l_s[...]
        for hh in range(DIFF_HEADS):
            o = acc[hh * hrows:hh * hrows + tnew] - lam * acc[hh * hrows + tnew:(hh + 1) * hrows]
            o_ref[:, hh * DIFF_V_DIM:(hh + 1) * DIFF_V_DIM] = _head_norm(o, g_ref[...], lam_init).astype(o_ref.dtype)


def _dattn_decode(h, k_pool, v_pool, page_table, w, layer, lam_init, batch, tnew):
    n_pool = k_pool.shape[1]
    n_pages = page_table.shape[1]
    kv_rows = DIFF_HEADS * 2
    kc = k_pool[layer].reshape(n_pool * PAGE_SIZE * kv_rows, DIFF_HEAD_DIM)
    vc = jnp.transpose(v_pool[layer], (0, 2, 1, 3))
    lam_spec = pl.BlockSpec((None, 1, DIFF_HEAD_DIM), lambda b, p, pt: (layer, 0, 0))
    row3 = _row3
    new_spec = lambda blk: pl.BlockSpec((tnew, MIX_WIDTH), lambda b, p, pt: (b, blk))
    k_spec = lambda g: pl.BlockSpec((PAGE_SIZE * kv_rows, DIFF_HEAD_DIM),
                                    lambda b, p, pt: (pt[b, p * DECODE_PAGES + g], 0))
    v_spec = lambda g: pl.BlockSpec((None, DIFF_HEADS, PAGE_SIZE, DIFF_V_DIM),
                                    lambda b, p, pt: (pt[b, p * DECODE_PAGES + g], 0, 0, 0))
    state_rows = kv_rows * tnew
    grid_spec = pltpu.PrefetchScalarGridSpec(
        num_scalar_prefetch=1,
        grid=(batch, n_pages // DECODE_PAGES),
        in_specs=[
            lam_spec, lam_spec, lam_spec, lam_spec,
            pl.BlockSpec((None, 1, DIFF_V_DIM), lambda b, p, pt: (layer, 0, 0)),
            new_spec(0), new_spec(1), new_spec(2),
            *[k_spec(g) for g in range(DECODE_PAGES)],
            *[v_spec(g) for g in range(DECODE_PAGES)],
        ],
        out_specs=pl.BlockSpec((tnew, MIX_WIDTH), lambda b, p, pt: (b, 0)),
        scratch_shapes=[
            pltpu.VMEM((state_rows, 1), F32),
            pltpu.VMEM((state_rows, 1), F32),
            pltpu.VMEM((state_rows, DIFF_V_DIM), F32),
            pltpu.VMEM((tnew, MIX_WIDTH), BF16),
        ],
    )
    return pl.pallas_call(
        functools.partial(_dattn_decode_kernel, tnew=tnew, lam_init=lam_init),
        grid_spec=grid_spec,
        out_shape=jax.ShapeDtypeStruct((batch * tnew, MIX_WIDTH), _act_dtype(tnew)),
        compiler_params=_params("parallel", "arbitrary"),
        name="diff_attn_decode",
    )(page_table, row3(w['diff_lambda_q1']), row3(w['diff_lambda_k1']), row3(w['diff_lambda_q2']),
      row3(w['diff_lambda_k2']), row3(w['diff_subln_g']), h, h, h, *([kc] * DECODE_PAGES), *([vc] * DECODE_PAGES))


def _mem_attn_kernel(q_ref, mk_ref, mv_ref, o_ref):
    s = _dot_nt(q_ref[...].astype(BF16), mk_ref[...].astype(BF16)) * (MEM_HEAD_DIM ** -0.5)
    p = jnp.exp(s - jnp.max(s, axis=-1, keepdims=True))
    l = jnp.sum(p, axis=-1, keepdims=True)
    o_ref[...] = (_dot(p.astype(BF16), mv_ref[...].astype(BF16)) / l).astype(o_ref.dtype)


def _mem_attn(h, q_col, mem_kv, batch, slen, *, ts):
    ns = slen // ts
    return pl.pallas_call(
        _mem_attn_kernel,
        grid=(batch, ns, MEM_HEADS),
        in_specs=[
            pl.BlockSpec((ts, MEM_HEAD_DIM), lambda b, si, hh: (b * ns + si, q_col + hh)),
            pl.BlockSpec((MEM_TOKENS, MEM_HEAD_DIM), lambda b, si, hh: (b, hh)),
            pl.BlockSpec((MEM_TOKENS, MEM_HEAD_DIM), lambda b, si, hh: (b, MEM_HEADS + hh)),
        ],
        out_specs=pl.BlockSpec((ts, MEM_HEAD_DIM), lambda b, si, hh: (b * ns + si, hh)),
        out_shape=jax.ShapeDtypeStruct((batch * slen, MEM_WIDTH), _act_dtype(ts)),
        compiler_params=_params("parallel", "parallel", "parallel"),
        name="mem_attn",
    )(h, mem_kv, mem_kv)


def _k_rows_kernel(x_ref, o_ref):
    tm = x_ref.shape[0]
    slots = x_ref.shape[1] // DIFF_HEAD_DIM
    for j in range(slots):
        o_ref[pl.ds(j, tm, stride=slots), :] = x_ref[:, j * DIFF_HEAD_DIM:(j + 1) * DIFF_HEAD_DIM]


def _k_rows(h, batch, slen, *, tm):
    m = h.shape[0]
    slots = MIX_WIDTH // DIFF_HEAD_DIM
    out = pl.pallas_call(
        _k_rows_kernel,
        grid=(m // tm,),
        in_specs=[pl.BlockSpec((tm, MIX_WIDTH), lambda i: (i, 1))],
        out_specs=pl.BlockSpec((tm * slots, DIFF_HEAD_DIM), lambda i: (i, 0)),
        out_shape=jax.ShapeDtypeStruct((m * slots, DIFF_HEAD_DIM), F32),
        compiler_params=_params("parallel"),
        name="k_rows",
    )(h)
    return out.reshape(1, batch, slen, DIFF_HEADS, 2, DIFF_HEAD_DIM)


def _v_heads_kernel(x_ref, o_ref):
    for hh in range(DIFF_HEADS):
        o_ref[hh] = x_ref[:, hh * DIFF_V_DIM:(hh + 1) * DIFF_V_DIM]


def _v_heads(h, batch, slen, *, tm):
    nt = slen // tm
    out = pl.pallas_call(
        _v_heads_kernel,
        grid=(batch, nt),
        in_specs=[pl.BlockSpec((tm, MIX_WIDTH), lambda b, i: (b * nt + i, 2))],
        out_specs=pl.BlockSpec((None, DIFF_HEADS, tm, DIFF_V_DIM), lambda b, i: (b, 0, i, 0)),
        out_shape=jax.ShapeDtypeStruct((batch, DIFF_HEADS, slen, DIFF_V_DIM), F32),
        compiler_params=_params("parallel", "parallel"),
        name="v_heads",
    )(h)
    return jnp.transpose(out, (0, 2, 1, 3)).reshape(1, batch, slen, DIFF_HEADS, DIFF_V_DIM)


def _tiles(m, streaming):
    if streaming:
        return dict(mm=dict(tm=m, tn=1024, tk=2048), mm_bf=dict(tm=m, tn=1024, tk=2048),
                    glu=dict(tm=m, tn=1024, tk=1024), ln=dict(tm=m, tn=256), ffn=dict(tm=m, tf=256))
    return dict(mm=dict(tm=2048, tn=1024, tk=1024), mm_bf=dict(tm=2048, tn=1024, tk=2048),
                glu=dict(tm=1024, tn=1024, tk=1024), ln=dict(tm=512, tn=256), ffn=dict(tm=1024, tf=256))


def _run_trunk(x, batch, slen, mem_kv, ssm_state, attn_past, w, wb):
    m = batch * slen
    emit = wb is None
    t = _tiles(m, emit)
    ts = min(slen, 512)
    new_k = new_v = s_re = s_im = None
    copies = {}

    def in_proj(x_in, tensor, li, i):
        tiles = t['mm_bf'] if x_in.dtype == BF16 else t['mm']
        if emit:
            h, copies['in', i] = _matmul(x_in, tensor, li, emit=True, **tiles)
            return h
        return _matmul(x_in, wb['in', i], **tiles)

    def proj_ln(name, a, tensor, x_res, g, b, i):
        if emit:
            xf, xb, copies[name, i] = _matmul_ln(a, tensor, x_res, g, b, i, emit=True, **t['ln'])
            return xf, xb
        return _matmul_ln(a, wb[name, i], x_res, g, b, i, **t['ln'])

    x_f32, x_in = x, x
    for i in range(DEPTH):
        li = i // 2
        if i % 2 == 0:
            h = in_proj(x_in, w['w_in_ssm'], li, i)
            if ssm_state is None:
                s0_re = s0_im = jnp.zeros((batch, S5_BLOCKS, 1, S5_HALF), F32)
            else:
                s0_re = ssm_state[0][li].reshape(batch, S5_BLOCKS, 1, S5_HALF)
                s0_im = ssm_state[1][li].reshape(batch, S5_BLOCKS, 1, S5_HALF)
            y, s_re, s_im = _s5(h, w['s5_tables'][li], s0_re, s0_im, batch, slen)
            if emit:
                mix, copies['glu', i] = _glu(y, w['w_glu'], w['b_glu'], li, _act_dtype(ts), emit=True, **t['glu'])
            else:
                mix = _glu(y, wb['glu', i], w['b_glu'], li, _act_dtype(ts), **t['glu'])
            q_col = MIX_WIDTH // MEM_HEAD_DIM
        else:
            h = in_proj(x_in, w['w_in_attn'], li, i)
            lam_init = 0.8 - 0.6 * math.exp(-0.3 * i)
            if attn_past is None:
                mix = _dattn_prompt(h, w, li, lam_init, batch, slen, tq=PROMPT_Q_BLOCK)
            else:
                mix = _dattn_decode(h, attn_past[0], attn_past[1], attn_past[2], w, li, lam_init, batch, slen)
            new_k = _k_rows(h, batch, slen, tm=min(m, 256))
            new_v = _v_heads(h, batch, slen, tm=min(slen, 256))
            q_col = 3 * MIX_WIDTH // MEM_HEAD_DIM
        mem = _mem_attn(h, q_col, mem_kv[i], batch, slen, ts=ts)
        x_f32, x_bf = proj_ln('o', (mix, mem), w['w_o'], x_f32, w['ln1_g'], w['ln1_b'], i)
        if emit:
            hid, copies['ffn_in', i] = _ffn_in(x_bf, w['w_ffn_in'], i, emit=True, **t['ffn'])
        else:
            hid = _ffn_in(x_bf, wb['ffn_in', i], **t['ffn'])
        x_f32, x_in = proj_ln('ffn_out', (hid,), w['w_ffn_out'], x_f32, w['ln2_g'], w['ln2_b'], i)
    return (x_f32, new_k, new_v, s_re, s_im), copies


def kernel(x_prompt, x_sample, cache_attn_k, cache_attn_v, state_ssm_re, state_ssm_im, cache_mem_k, cache_mem_v,
           page_table, mem_prompt, w_in_ssm, ssm_lambda_re, ssm_lambda_im, ssm_log_dt, ssm_b_re, ssm_b_im,
           ssm_c_re, ssm_c_im, ssm_d, w_glu, b_glu, w_in_attn, diff_lambda_q1, diff_lambda_k1, diff_lambda_q2,
           diff_lambda_k2, diff_subln_g, w_mem_kv, w_o, ln1_g, ln1_b, w_ffn_in, w_ffn_out, ln2_g, ln2_b):
    w = dict(w_in_ssm=w_in_ssm, ssm_lambda_re=ssm_lambda_re, ssm_lambda_im=ssm_lambda_im, ssm_log_dt=ssm_log_dt,
             ssm_b_re=ssm_b_re, ssm_b_im=ssm_b_im, ssm_c_re=ssm_c_re, ssm_c_im=ssm_c_im, ssm_d=ssm_d,
             w_glu=w_glu, b_glu=b_glu, w_in_attn=w_in_attn, diff_lambda_q1=diff_lambda_q1,
             diff_lambda_k1=diff_lambda_k1, diff_lambda_q2=diff_lambda_q2, diff_lambda_k2=diff_lambda_k2,
             diff_subln_g=diff_subln_g, w_o=w_o, ln1_g=ln1_g, ln1_b=ln1_b, w_ffn_in=w_ffn_in,
             w_ffn_out=w_ffn_out, ln2_g=ln2_g, ln2_b=ln2_b)
    bp, sp, _ = x_prompt.shape
    bs, ss, _ = x_sample.shape

    mem_rows = mem_prompt.reshape(bp * MEM_TOKENS, D_MODEL)
    mem_kv_p = [_matmul(mem_rows, w_mem_kv, l, tm=bp * MEM_TOKENS, tn=1024, tk=1024) for l in range(DEPTH)]
    mem_k_prompt = jnp.stack([kv[:, :MEM_WIDTH].reshape(bp, MEM_TOKENS, MEM_HEADS, MEM_HEAD_DIM) for kv in mem_kv_p])
    mem_v_prompt = jnp.stack([kv[:, MEM_WIDTH:].reshape(bp, MEM_TOKENS, MEM_HEADS, MEM_HEAD_DIM) for kv in mem_kv_p])
    mem_kv_s = [jnp.concatenate([cache_mem_k[l].reshape(bs * MEM_TOKENS, MEM_WIDTH),
                                 cache_mem_v[l].reshape(bs * MEM_TOKENS, MEM_WIDTH)], axis=1) for l in range(DEPTH)]

    w['s5_tables'] = [_s5_tables(ssm_lambda_re[l], ssm_lambda_im[l], ssm_log_dt[l], ssm_b_re[l], ssm_b_im[l],
                                 ssm_c_re[l], ssm_c_im[l], ssm_d[l]) for l in range(ssm_d.shape[0])]

    (ys, ks, vs, srs, sis), wb = _run_trunk(x_sample.reshape(bs * ss, D_MODEL), bs, ss, mem_kv_s,
                                            (state_ssm_re, state_ssm_im),
                                            (cache_attn_k, cache_attn_v, page_table), w, None)
    (yp, kp, vp, srp, sip), _ = _run_trunk(x_prompt.reshape(bp * sp, D_MODEL), bp, sp, mem_kv_p, None, None, w, wb)

    def sshape(s, b):
        return s.reshape(1, b, SSM_GROUPS, SSM_STATE)

    return (yp.reshape(bp, sp, D_MODEL), ys.reshape(bs, ss, D_MODEL), kp, vp, ks, vs,
            sshape(srp, bp), sshape(sip, bp), sshape(srs, bs), sshape(sis, bs),
            mem_k_prompt, mem_v_prompt)
```

```python
import functools
import math

import jax
import jax.numpy as jnp
from jax import lax
from jax.experimental import pallas as pl
from jax.experimental.pallas import tpu as pltpu

F32 = jnp.float32
BF16 = jnp.bfloat16

D_MODEL = 4096
DEPTH = 2
MEM_TOKENS = 256
MEM_HEADS = 4
MEM_HEAD_DIM = 256
MEM_WIDTH = MEM_HEADS * MEM_HEAD_DIM
MIX_WIDTH = D_MODEL - MEM_WIDTH
SSM_GROUP = 16
SSM_GROUPS = MIX_WIDTH // SSM_GROUP
SSM_STATE = 64
DIFF_HEAD_DIM = 128
DIFF_HEADS = MIX_WIDTH // (2 * DIFF_HEAD_DIM)
DIFF_V_DIM = 2 * DIFF_HEAD_DIM
D_FF = 11008
ALPHA = (2 * DEPTH) ** 0.25
LN_EPS = 1e-5
PAGE_SIZE = 128

VMEM_LIMIT_BYTES = 58 * 1024 * 1024
LANES = 128
SUBLANES = 8

S5_GROUPS_PER_BLOCK = 16
S5_CH = S5_GROUPS_PER_BLOCK * SSM_GROUP
S5_HALF = S5_GROUPS_PER_BLOCK * SSM_STATE
S5_BLOCKS = SSM_GROUPS // S5_GROUPS_PER_BLOCK
S5_SEG = 64
S5_PITCH = S5_SEG + SUBLANES
LN_ROWS = 128
DECODE_PAGES = 4
PROMPT_Q_BLOCK = 512


def _params(*sem):
    return pltpu.CompilerParams(dimension_semantics=sem, vmem_limit_bytes=VMEM_LIMIT_BYTES)


def _dot(a, b):
    return jnp.dot(a, b, preferred_element_type=F32)


def _dot_nt(a, b):
    return lax.dot_general(a, b, (((1,), (1,)), ((), ())), preferred_element_type=F32)


def _w_spec(w, layer, block, index):
    if w.ndim == 3:
        return pl.BlockSpec((None,) + block, lambda *g: (layer,) + index(*g))
    return pl.BlockSpec(block, index)


def _accumulate(ref, first, term):
    @pl.when(first)
    def _():
        ref[...] = term()

    @pl.when(jnp.logical_not(first))
    def _():
        ref[...] += term()


def _mm_kernel(x_ref, w_ref, o_ref, *wb_ref):
    w = w_ref[...].astype(BF16)
    if wb_ref:
        wb_ref[0][...] = w
    _accumulate(o_ref, pl.program_id(2) == 0, lambda: _dot(x_ref[...].astype(BF16), w))


def _matmul(x, w, layer=None, *, tm, tn, tk, emit=False):
    m, k = x.shape
    n = w.shape[-1]
    assert not emit or m == tm
    out_specs = [pl.BlockSpec((tm, tn), lambda i, j, kk: (i, j))]
    out_shape = [jax.ShapeDtypeStruct((m, n), F32)]
    if emit:
        out_specs.append(pl.BlockSpec((tk, tn), lambda i, j, kk: (kk, j)))
        out_shape.append(jax.ShapeDtypeStruct((k, n), BF16))
    res = pl.pallas_call(
        _mm_kernel,
        grid=(m // tm, n // tn, k // tk),
        in_specs=[
            pl.BlockSpec((tm, tk), lambda i, j, kk: (i, kk)),
            _w_spec(w, layer, (tk, tn), lambda i, j, kk: (kk, j)),
        ],
        out_specs=out_specs,
        out_shape=out_shape,
        compiler_params=_params("parallel", "parallel", "arbitrary"),
        name="matmul",
    )(x, w)
    return res if emit else res[0]


def _glu_kernel(yk_ref, w_ref, b_ref, yn_ref, o_ref, *rest):
    acc_ref = rest[-1]
    kk = pl.program_id(2)
    w = w_ref[...].astype(BF16)
    if len(rest) == 2:
        rest[0][...] = w
    _accumulate(acc_ref, kk == 0, lambda: _dot(yk_ref[...].astype(BF16), w))

    @pl.when(kk == pl.num_programs(2) - 1)
    def _():
        gate = jax.nn.sigmoid(acc_ref[...] + b_ref[...])
        o_ref[...] = (yn_ref[...] * gate).astype(o_ref.dtype)


def _act_dtype(rows_per_block):
    return BF16 if rows_per_block % (2 * SUBLANES) == 0 else F32


def _glu(y, w, b, layer, out_dtype, *, tm, tn, tk, emit=False):
    m, k = y.shape
    n = w.shape[-1]
    assert not emit or m == tm
    out_specs = [pl.BlockSpec((tm, tn), lambda i, j, kk: (i, j))]
    out_shape = [jax.ShapeDtypeStruct((m, n), out_dtype)]
    if emit:
        out_specs.append(pl.BlockSpec((tk, tn), lambda i, j, kk: (kk, j)))
        out_shape.append(jax.ShapeDtypeStruct((k, n), BF16))
    res = pl.pallas_call(
        _glu_kernel,
        grid=(m // tm, n // tn, k // tk),
        in_specs=[
            pl.BlockSpec((tm, tk), lambda i, j, kk: (i, kk)),
            _w_spec(w, layer, (tk, tn), lambda i, j, kk: (kk, j)),
            pl.BlockSpec((None, 1, tn), lambda i, j, kk: (layer, 0, j)),
            pl.BlockSpec((tm, tn), lambda i, j, kk: (i, j)),
        ],
        out_specs=out_specs,
        out_shape=out_shape,
        scratch_shapes=[pltpu.VMEM((tm, tn), F32)],
        compiler_params=_params("parallel", "parallel", "arbitrary"),
        name="glu",
    )(y, w, b.reshape(b.shape[0], 1, b.shape[1]), y)
    return res if emit else res[0]


def _ln_epilogue(result, xs, g_ref, b_ref, of_ref, ob_ref):
    panels, tm, tn = xs.shape
    rc = min(tm, LN_ROWS)
    width = panels * tn

    def rows(c, _):
        r = pl.ds(pl.multiple_of(c * rc, rc), rc)
        cols = [slice(p * tn, (p + 1) * tn) for p in range(panels)]
        ys = [ALPHA * xs[p, r, :] + result(p, r, cs) for p, cs in enumerate(cols)]
        mu = sum(jnp.sum(y, axis=-1, keepdims=True) for y in ys) / width
        ds = [y - mu for y in ys]
        var = sum(jnp.sum(d * d, axis=-1, keepdims=True) for d in ds) / width
        inv = lax.rsqrt(var + LN_EPS)
        for d, cs in zip(ds, cols):
            out = d * inv * g_ref[:, cs] + b_ref[:, cs]
            of_ref[r, cs] = out
            ob_ref[r, cs] = out.astype(BF16)
        return 0

    lax.fori_loop(0, tm // rc, rows, 0)


def _mm_ln_kernel(*refs, n_a):
    a_refs = refs[:n_a]
    w_ref, xp_ref, g_ref, b_ref, of_ref, ob_ref, *rest = refs[n_a:]
    acc, xs = rest[-2:]
    j = pl.program_id(1)
    w = w_ref[...].astype(BF16)
    if len(rest) == 3:
        rest[0][...] = w
    total, k0 = None, 0
    for a_ref in a_refs:
        k1 = k0 + a_ref.shape[1]
        term = _dot(a_ref[...].astype(BF16), w[k0:k1])
        total = term if total is None else total + term
        k0 = k1
    acc[j] = total
    xs[j] = xp_ref[...]

    @pl.when(j == pl.num_programs(1) - 1)
    def _():
        _ln_epilogue(lambda p, r, cs: acc[p, r, :], xs, g_ref, b_ref, of_ref, ob_ref)


def _panel_spec(w, layer, k, tn):
    if w.dtype == BF16:
        assert w.shape[1:] == (k, tn)
        return pl.BlockSpec((None, k, tn), lambda i, j: (j, 0, 0))
    return pl.BlockSpec((None, k, tn), lambda i, j: (layer, 0, j))


def _matmul_ln(a_parts, w, x, g, b, layer, *, tm, tn, emit=False, tight=False):
    m, n = x.shape
    k = sum(a.shape[1] for a in a_parts)
    assert not emit or m == tm
    mode = pl.Buffered(1) if tight else None
    out_specs = [pl.BlockSpec((tm, n), lambda i, j: (i, 0), pipeline_mode=mode),
                 pl.BlockSpec((tm, n), lambda i, j: (i, 0))]
    out_shape = [jax.ShapeDtypeStruct((m, n), F32), jax.ShapeDtypeStruct((m, n), BF16)]
    if emit:
        out_specs.append(pl.BlockSpec((None, k, tn), lambda i, j: (j, 0, 0)))
        out_shape.append(jax.ShapeDtypeStruct((n // tn, k, tn), BF16))
    return pl.pallas_call(
        functools.partial(_mm_ln_kernel, n_a=len(a_parts)),
        grid=(m // tm, n // tn),
        in_specs=[
            *[pl.BlockSpec((tm, a.shape[1]), lambda i, j: (i, 0), pipeline_mode=mode) for a in a_parts],
            _panel_spec(w, layer, k, tn),
            pl.BlockSpec((tm, tn), lambda i, j: (i, j)),
            pl.BlockSpec((None, 1, n), lambda i, j: (layer, 0, 0)),
            pl.BlockSpec((None, 1, n), lambda i, j: (layer, 0, 0)),
        ],
        out_specs=out_specs,
        out_shape=out_shape,
        scratch_shapes=[pltpu.VMEM((n // tn, tm, tn), F32), pltpu.VMEM((n // tn, tm, tn), F32)],
        compiler_params=_params("parallel", "arbitrary"),
        name="matmul_ln",
    )(*a_parts, w, x, g.reshape(g.shape[0], 1, n), b.reshape(b.shape[0], 1, n))


def _ffn_in_kernel(x_ref, wg_ref, wu_ref, o_ref, *wb_refs):
    x = x_ref[...]
    wg = wg_ref[...].astype(BF16)
    wu = wu_ref[...].astype(BF16)
    if wb_refs:
        wb_refs[0][...] = wg
        wb_refs[1][...] = wu
    o_ref[...] = (jax.nn.silu(_dot(x, wg)) * _dot(x, wu)).astype(o_ref.dtype)


def _ffn_in(x, w, layer=None, *, tm, tf, emit=False):
    m, k = x.shape
    nf = D_FF // tf
    assert not emit or m == tm
    if isinstance(w, tuple):
        wg, wu = w
        w_specs = [_panel_spec(wg, None, k, tf), _panel_spec(wu, None, k, tf)]
    else:
        wg = wu = w
        w_specs = [pl.BlockSpec((None, k, tf), lambda i, j: (layer, 0, j)),
                   pl.BlockSpec((None, k, tf), lambda i, j: (layer, 0, j + nf))]
    out_specs = [pl.BlockSpec((tm, tf), lambda i, j: (i, j))]
    out_shape = [jax.ShapeDtypeStruct((m, D_FF), BF16)]
    if emit:
        out_specs += [pl.BlockSpec((None, k, tf), lambda i, j: (j, 0, 0))] * 2
        out_shape += [jax.ShapeDtypeStruct((nf, k, tf), BF16)] * 2
    res = pl.pallas_call(
        _ffn_in_kernel,
        grid=(m // tm, nf),
        in_specs=[pl.BlockSpec((tm, k), lambda i, j: (i, 0), pipeline_mode=pl.Buffered(1))] + w_specs,
        out_specs=out_specs,
        out_shape=out_shape,
        compiler_params=_params("parallel", "arbitrary"),
        name="ffn_in",
    )(x, wg, wu)
    return (res[0], (res[1], res[2])) if emit else res[0]


def _s5_kernel(u_ref, bb_ref, cb_ref, are_ref, aim_ref, apre_ref, apim_ref, d_ref, s0re_ref, s0im_ref,
               y_ref, sre_ref, sim_ref,
               uslab, uperm, bu, yslab, cre, cim, *, seg, pitch):
    half = S5_HALF

    @pl.when(pl.program_id(2) == 0)
    def _():
        cre[...] = s0re_ref[...]
        cim[...] = s0im_ref[...]

    if seg == 1:
        uperm[...] = u_ref[...]
    else:
        for i in range(SUBLANES):
            for j in range(S5_CH // LANES):
                uslab[j, i * pitch:i * pitch + seg, :] = u_ref[i * seg:(i + 1) * seg, j * LANES:(j + 1) * LANES]

        def gather(r, _):
            row = pl.ds(pl.multiple_of(r * SUBLANES, SUBLANES), SUBLANES)
            for j in range(S5_CH // LANES):
                uperm[row, j * LANES:(j + 1) * LANES] = uslab[j, pl.ds(r, SUBLANES, stride=pitch), :]
            return 0

        lax.fori_loop(0, seg, gather, 0, unroll=min(seg, 4))

    up = uperm[...]
    bu[...] = _dot(up.astype(BF16), bb_ref[...])

    are = jnp.broadcast_to(are_ref[...], (SUBLANES, half))
    aim = jnp.broadcast_to(aim_ref[...], (SUBLANES, half))

    def advance(r, sre, sim):
        row = pl.ds(pl.multiple_of(r * SUBLANES, SUBLANES), SUBLANES)
        nre = are * sre - aim * sim + bu[row, :half]
        nim = are * sim + aim * sre + bu[row, half:]
        return row, nre, nim

    def local_step(r, carry):
        _, nre, nim = advance(r, *carry)
        return nre, nim

    zero = jnp.zeros((SUBLANES, half), F32)
    ere, eim = lax.fori_loop(0, seg, local_step, (zero, zero), unroll=min(seg, 2))

    apre = apre_ref[...]
    apim = apim_ref[...]
    rows_re = [cre[...]]
    rows_im = [cim[...]]
    for i in range(1, SUBLANES + 1):
        pre, pim = rows_re[-1], rows_im[-1]
        rows_re.append(ere[i - 1:i] + apre * pre - apim * pim)
        rows_im.append(eim[i - 1:i] + apre * pim + apim * pre)
    cre[...] = rows_re[-1]
    cim[...] = rows_im[-1]
    sre_ref[...] = rows_re[-1]
    sim_ref[...] = rows_im[-1]
    cin_re = jnp.concatenate(rows_re[:SUBLANES], axis=0)
    cin_im = jnp.concatenate(rows_im[:SUBLANES], axis=0)

    def true_step(r, carry):
        row, nre, nim = advance(r, *carry)
        bu[row, :half] = nre
        bu[row, half:] = nim
        return nre, nim

    lax.fori_loop(0, seg, true_step, (cin_re, cin_im), unroll=min(seg, 2))

    yp = _dot(bu[...].astype(BF16), cb_ref[...]) + d_ref[...] * up
    yp = jax.nn.gelu(yp)
    if seg == 1:
        y_ref[...] = yp
    else:
        for j in range(S5_CH // LANES):
            yslab[j] = yp[:, j * LANES:(j + 1) * LANES]

        def scatter(q, _):
            for i in range(SUBLANES):
                dst = pl.ds(pl.multiple_of(i * seg + q * SUBLANES, SUBLANES), SUBLANES)
                src = pl.ds(q * SUBLANES * SUBLANES + i, SUBLANES, stride=SUBLANES)
                for j in range(S5_CH // LANES):
                    y_ref[dst, j * LANES:(j + 1) * LANES] = yslab[j, src, :]
            return 0

        lax.fori_loop(0, seg // SUBLANES, scatter, 0)


def _s5_tables(lam_re, lam_im, log_dt, b_re, b_im, c_re, c_im, d_skip):
    dt = jnp.exp(log_dt)[:, None]
    mag = jnp.exp(lam_re * dt)
    ab_re = mag * jnp.cos(lam_im * dt)
    ab_im = mag * jnp.sin(lam_im * dt)
    den = lam_re * lam_re + lam_im * lam_im
    nr = ab_re - 1.0
    f_re = (nr * lam_re + ab_im * lam_im) / den
    f_im = (ab_im * lam_re - nr * lam_im) / den
    bb_re = f_re[..., None] * b_re - f_im[..., None] * b_im
    bb_im = f_re[..., None] * b_im + f_im[..., None] * b_re
    eye = jnp.eye(S5_GROUPS_PER_BLOCK, dtype=F32)
    gpb = S5_GROUPS_PER_BLOCK

    def expand_b(bb):
        bb = bb.reshape(S5_BLOCKS, gpb, SSM_STATE, SSM_GROUP)
        return jnp.einsum('bgpc,gh->bgchp', bb, eye).reshape(S5_BLOCKS, S5_CH, S5_HALF)

    def expand_c(cc):
        cc = cc.reshape(S5_BLOCKS, gpb, SSM_GROUP, SSM_STATE)
        return jnp.einsum('bgcp,gh->bgphc', cc, eye).reshape(S5_BLOCKS, S5_HALF, S5_CH)

    bblk = jnp.concatenate([expand_b(bb_re), expand_b(bb_im)], axis=2).astype(BF16)
    cblk = jnp.concatenate([expand_c(c_re), -expand_c(c_im)], axis=1).astype(BF16)
    vec = lambda v: v.reshape(S5_BLOCKS, 1, S5_HALF)
    return bblk, cblk, vec(ab_re), vec(ab_im), d_skip.reshape(S5_BLOCKS, 1, S5_CH)


def _s5(h, tables, s0_re, s0_im, batch, slen):
    bblk, cblk, are, aim, dsk = tables
    seg = S5_SEG if slen >= SUBLANES * S5_SEG else slen // SUBLANES
    apre, apim = are, aim
    for _ in range(int(math.log2(seg))):
        apre, apim = apre * apre - apim * apim, 2.0 * apre * apim
    tc = SUBLANES * seg
    nchunk = slen // tc
    pitch = S5_PITCH
    vec = pl.BlockSpec((None, 1, S5_HALF), lambda g, b, c: (g, 0, 0))
    state = pl.BlockSpec((None, None, 1, S5_HALF), lambda g, b, c: (b, g, 0, 0))
    y, sre, sim = pl.pallas_call(
        functools.partial(_s5_kernel, seg=seg, pitch=pitch),
        grid=(S5_BLOCKS, batch, nchunk),
        in_specs=[
            pl.BlockSpec((tc, S5_CH), lambda g, b, c: (b * nchunk + c, g)),
            pl.BlockSpec((None, S5_CH, 2 * S5_HALF), lambda g, b, c: (g, 0, 0)),
            pl.BlockSpec((None, 2 * S5_HALF, S5_CH), lambda g, b, c: (g, 0, 0)),
            vec, vec, vec, vec,
            pl.BlockSpec((None, 1, S5_CH), lambda g, b, c: (g, 0, 0)),
            state, state,
        ],
        out_specs=[
            pl.BlockSpec((tc, S5_CH), lambda g, b, c: (b * nchunk + c, g)),
            state, state,
        ],
        out_shape=[
            jax.ShapeDtypeStruct((batch * slen, MIX_WIDTH), F32),
            jax.ShapeDtypeStruct((batch, S5_BLOCKS, 1, S5_HALF), F32),
            jax.ShapeDtypeStruct((batch, S5_BLOCKS, 1, S5_HALF), F32),
        ],
        scratch_shapes=[
            pltpu.VMEM((S5_CH // LANES, SUBLANES * pitch, LANES), F32),
            pltpu.VMEM((tc, S5_CH), F32),
            pltpu.VMEM((tc, 2 * S5_HALF), F32),
            pltpu.VMEM((S5_CH // LANES, tc, LANES), F32),
            pltpu.VMEM((1, S5_HALF), F32),
            pltpu.VMEM((1, S5_HALF), F32),
        ],
        compiler_params=_params("parallel", "parallel", "arbitrary"),
        name="s5_mixer",
    )(h, bblk, cblk, are, aim, apre, apim, dsk, s0_re, s0_im)
    return y, sre, sim


def _diff_lambda(lq1, lk1, lq2, lk2, lam_init):
    e1 = jnp.exp(jnp.sum(lq1[...] * lk1[...], axis=-1, keepdims=True))
    e2 = jnp.exp(jnp.sum(lq2[...] * lk2[...], axis=-1, keepdims=True))
    return e1 - e2 + lam_init


def _softmax_update(s, v, m, l, acc):
    m_new = jnp.maximum(m, jnp.max(s, axis=-1, keepdims=True))
    alpha = jnp.exp(m - m_new)
    p = jnp.exp(s - m_new)
    l_new = alpha * l + jnp.sum(p, axis=-1, keepdims=True)
    acc_new = alpha * acc + _dot(p.astype(BF16), v)
    return m_new, l_new, acc_new


def _head_norm(o, g, lam_init):
    o = o * lax.rsqrt(jnp.mean(o * o, axis=-1, keepdims=True) + LN_EPS) * g
    return o * (1.0 - lam_init)


def _dattn_prompt_kernel(lq1, lk1, lq2, lk2, g_ref, q_ref, k_ref, v_ref, o_ref, k1s, k2s, vs, *, tq, lam_init):
    qi = pl.program_id(2)
    d = DIFF_HEAD_DIM
    scale = d ** -0.5
    update = _softmax_update

    @pl.when(qi == 0)
    def _():
        k1s[...] = k_ref[:, :d].astype(BF16)
        k2s[...] = k_ref[:, d:].astype(BF16)
        vs[...] = v_ref[...].astype(BF16)

    lam = _diff_lambda(lq1, lk1, lq2, lk2, lam_init)
    q1 = q_ref[:, :d].astype(BF16)
    q2 = q_ref[:, d:].astype(BF16)

    def block(j):
        rows = pl.ds(pl.multiple_of(j * tq, tq), tq)
        return _dot_nt(q1, k1s[rows, :]) * scale, _dot_nt(q2, k2s[rows, :]) * scale, vs[rows, :]

    def body(j, st):
        s1, s2, v = block(j)
        return update(s1, v, *st[:3]) + update(s2, v, *st[3:])

    init = (jnp.full((tq, 1), -jnp.inf, F32), jnp.zeros((tq, 1), F32), jnp.zeros((tq, DIFF_V_DIM), F32))
    st = lax.fori_loop(0, qi, body, init + init)

    s1, s2, v = block(qi)
    causal = lax.broadcasted_iota(jnp.int32, (tq, tq), 0) >= lax.broadcasted_iota(jnp.int32, (tq, tq), 1)
    s1 = jnp.where(causal, s1, -jnp.inf)
    s2 = jnp.where(causal, s2, -jnp.inf)
    _, l1, a1 = update(s1, v, *st[:3])
    _, l2, a2 = update(s2, v, *st[3:])
    o = a1 / l1 - lam * (a2 / l2)
    o_ref[...] = _head_norm(o, g_ref[...], lam_init).astype(o_ref.dtype)


def _row3(v):
    return v.reshape(v.shape[0], 1, v.shape[1])


def _dattn_prompt(h, w, layer, lam_init, batch, slen, *, tq):
    nq = slen // tq
    lam_spec = pl.BlockSpec((None, 1, DIFF_HEAD_DIM), lambda b, hh, qi: (layer, 0, 0))
    row3 = _row3
    return pl.pallas_call(
        functools.partial(_dattn_prompt_kernel, tq=tq, lam_init=lam_init),
        grid=(batch, DIFF_HEADS, nq),
        in_specs=[
            lam_spec, lam_spec, lam_spec, lam_spec,
            pl.BlockSpec((None, 1, DIFF_V_DIM), lambda b, hh, qi: (layer, 0, 0)),
            pl.BlockSpec((tq, DIFF_V_DIM), lambda b, hh, qi: (b * nq + qi, hh)),
            pl.BlockSpec((slen, DIFF_V_DIM), lambda b, hh, qi: (b, DIFF_HEADS + hh)),
            pl.BlockSpec((slen, DIFF_V_DIM), lambda b, hh, qi: (b, 2 * DIFF_HEADS + hh)),
        ],
        out_specs=pl.BlockSpec((tq, DIFF_V_DIM), lambda b, hh, qi: (b * nq + qi, hh)),
        out_shape=jax.ShapeDtypeStruct((batch * slen, MIX_WIDTH), BF16),
        scratch_shapes=[
            pltpu.VMEM((slen, DIFF_HEAD_DIM), BF16),
            pltpu.VMEM((slen, DIFF_HEAD_DIM), BF16),
            pltpu.VMEM((slen, DIFF_V_DIM), BF16),
        ],
        compiler_params=_params("parallel", "parallel", "arbitrary"),
        name="diff_attn_prompt",
    )(row3(w['diff_lambda_q1']), row3(w['diff_lambda_k1']), row3(w['diff_lambda_q2']), row3(w['diff_lambda_k2']),
      row3(w['diff_subln_g']), h, h, h)


def _dattn_decode_kernel(pt_ref, lq1, lk1, lq2, lk2, g_ref, q_ref, kn_ref, vn_ref, *rest, tnew, lam_init):
    del pt_ref
    kc_refs = rest[:DECODE_PAGES]
    vc_refs = rest[DECODE_PAGES:2 * DECODE_PAGES]
    o_ref, m_s, l_s, acc_s, q_s = rest[2 * DECODE_PAGES:]
    p = pl.program_id(1)
    d = DIFF_HEAD_DIM
    scale = d ** -0.5
    kv_rows = DIFF_HEADS * 2
    hrows = 2 * tnew

    @pl.when(p == 0)
    def _():
        m_s[...] = jnp.full_like(m_s, -jnp.inf)
        l_s[...] = jnp.zeros_like(l_s)
        acc_s[...] = jnp.zeros_like(acc_s)
        q_s[...] = q_ref[...].astype(BF16)

    def q_of(j):
        return q_s[:, j * d:(j + 1) * d]

    def update(s, pv_of):
        m_old = m_s[...]
        m_new = jnp.maximum(m_old, jnp.max(s, axis=-1, keepdims=True))
        alpha = jnp.exp(m_old - m_new)
        prob = jnp.exp(s - m_new)
        l_s[...] = alpha * l_s[...] + jnp.sum(prob, axis=-1, keepdims=True)
        m_s[...] = m_new
        pb = prob.astype(BF16)
        pv = jnp.concatenate([pv_of(hh, pb[hh * hrows:(hh + 1) * hrows]) for hh in range(DIFF_HEADS)], axis=0)
        acc_s[...] = alpha * acc_s[...] + pv

    s_pages = []
    for kc_ref in kc_refs:
        s_pages.append(jnp.concatenate(
            [_dot_nt(q_of(j), kc_ref[pl.ds(j, PAGE_SIZE, stride=kv_rows), :].astype(BF16)) for j in range(kv_rows)],
            axis=0))
    s_all = jnp.concatenate(s_pages, axis=1) * scale

    def pv_pages(hh, pb):
        out = None
        for g, vc_ref in enumerate(vc_refs):
            term = _dot(pb[:, g * PAGE_SIZE:(g + 1) * PAGE_SIZE], vc_ref[hh].astype(BF16))
            out = term if out is None else out + term
        return out

    update(s_all, pv_pages)

    @pl.when(p == pl.num_programs(1) - 1)
    def _():
        lam = _diff_lambda(lq1, lk1, lq2, lk2, lam_init)
        pad_rows = jnp.zeros((LANES - tnew, d), BF16)
        pad_v = jnp.zeros((LANES - tnew, DIFF_V_DIM), BF16)
        row = lax.broadcasted_iota(jnp.int32, (tnew, LANES), 0)
        col = lax.broadcasted_iota(jnp.int32, (tnew, LANES), 1)
        causal = jnp.concatenate([col <= row] * kv_rows, axis=0)
        s_new = jnp.concatenate(
            [_dot_nt(q_of(j), jnp.concatenate([kn_ref[:, j * d:(j + 1) * d].astype(BF16), pad_rows], axis=0))
             for j in range(kv_rows)], axis=0) * scale

        def pv_new(hh, pb):
            v = jnp.concatenate([vn_ref[:, hh * DIFF_V_DIM:(hh + 1) * DIFF_V_DIM].astype(BF16), pad_v], axis=0)
            return _dot(pb, v)

        update(jnp.where(causal, s_new, -jnp.inf), pv_new)
        acc = acc_s[...] / l_s[...]
        for hh in range(DIFF_HEADS):
            o = acc[hh * hrows:hh * hrows + tnew] - lam * acc[hh * hrows + tnew:(hh + 1) * hrows]
            o_ref[:, hh * DIFF_V_DIM:(hh + 1) * DIFF_V_DIM] = _head_norm(o, g_ref[...], lam_init).astype(o_ref.dtype)


def _dattn_decode(h, k_pool, v_pool, page_table, w, layer, lam_init, batch, tnew):
    n_pool = k_pool.shape[1]
    n_pages = page_table.shape[1]
    kv_rows = DIFF_HEADS * 2
    kc = k_pool[layer].reshape(n_pool * PAGE_SIZE * kv_rows, DIFF_HEAD_DIM)
    vc = jnp.transpose(v_pool[layer], (0, 2, 1, 3))
    lam_spec = pl.BlockSpec((None, 1, DIFF_HEAD_DIM), lambda b, p, pt: (layer, 0, 0))
    row3 = _row3
    new_spec = lambda blk: pl.BlockSpec((tnew, MIX_WIDTH), lambda b, p, pt: (b, blk))
    k_spec = lambda g: pl.BlockSpec((PAGE_SIZE * kv_rows, DIFF_HEAD_DIM),
                                    lambda b, p, pt: (pt[b, p * DECODE_PAGES + g], 0))
    v_spec = lambda g: pl.BlockSpec((None, DIFF_HEADS, PAGE_SIZE, DIFF_V_DIM),
                                    lambda b, p, pt: (pt[b, p * DECODE_PAGES + g], 0, 0, 0))
    state_rows = kv_rows * tnew
    grid_spec = pltpu.PrefetchScalarGridSpec(
        num_scalar_prefetch=1,
        grid=(batch, n_pages // DECODE_PAGES),
        in_specs=[
            lam_spec, lam_spec, lam_spec, lam_spec,
            pl.BlockSpec((None, 1, DIFF_V_DIM), lambda b, p, pt: (layer, 0, 0)),
            new_spec(0), new_spec(1), new_spec(2),
            *[k_spec(g) for g in range(DECODE_PAGES)],
            *[v_spec(g) for g in range(DECODE_PAGES)],
        ],
        out_specs=pl.BlockSpec((tnew, MIX_WIDTH), lambda b, p, pt: (b, 0)),
        scratch_shapes=[
            pltpu.VMEM((state_rows, 1), F32),
            pltpu.VMEM((state_rows, 1), F32),
            pltpu.VMEM((state_rows, DIFF_V_DIM), F32),
            pltpu.VMEM((tnew, MIX_WIDTH), BF16),
        ],
    )
    return pl.pallas_call(
        functools.partial(_dattn_decode_kernel, tnew=tnew, lam_init=lam_init),
        grid_spec=grid_spec,
        out_shape=jax.ShapeDtypeStruct((batch * tnew, MIX_WIDTH), _act_dtype(tnew)),
        compiler_params=_params("parallel", "arbitrary"),
        name="diff_attn_decode",
    )(page_table, row3(w['diff_lambda_q1']), row3(w['diff_lambda_k1']), row3(w['diff_lambda_q2']),
      row3(w['diff_lambda_k2']), row3(w['diff_subln_g']), h, h, h, *([kc] * DECODE_PAGES), *([vc] * DECODE_PAGES))


def _mem_attn_kernel(q_ref, mk_ref, mv_ref, o_ref):
    s = _dot_nt(q_ref[...].astype(BF16), mk_ref[...].astype(BF16)) * (MEM_HEAD_DIM ** -0.5)
    p = jnp.exp(s - jnp.max(s, axis=-1, keepdims=True))
    l = jnp.sum(p, axis=-1, keepdims=True)
    o_ref[...] = (_dot(p.astype(BF16), mv_ref[...].astype(BF16)) / l).astype(o_ref.dtype)


def _mem_attn(h, q_col, mem_kv, batch, slen, *, ts):
    ns = slen // ts
    return pl.pallas_call(
        _mem_attn_kernel,
        grid=(batch, ns, MEM_HEADS),
        in_specs=[
            pl.BlockSpec((ts, MEM_HEAD_DIM), lambda b, si, hh: (b * ns + si, q_col + hh)),
            pl.BlockSpec((MEM_TOKENS, MEM_HEAD_DIM), lambda b, si, hh: (b, hh)),
            pl.BlockSpec((MEM_TOKENS, MEM_HEAD_DIM), lambda b, si, hh: (b, MEM_HEADS + hh)),
        ],
        out_specs=pl.BlockSpec((ts, MEM_HEAD_DIM), lambda b, si, hh: (b * ns + si, hh)),
        out_shape=jax.ShapeDtypeStruct((batch * slen, MEM_WIDTH), _act_dtype(ts)),
        compiler_params=_params("parallel", "parallel", "parallel"),
        name="mem_attn",
    )(h, mem_kv, mem_kv)


def _k_rows_kernel(x_ref, o_ref):
    tm = x_ref.shape[0]
    slots = x_ref.shape[1] // DIFF_HEAD_DIM
    for j in range(slots):
        o_ref[pl.ds(j, tm, stride=slots), :] = x_ref[:, j * DIFF_HEAD_DIM:(j + 1) * DIFF_HEAD_DIM]


def _k_rows(h, batch, slen, *, tm):
    m = h.shape[0]
    slots = MIX_WIDTH // DIFF_HEAD_DIM
    out = pl.pallas_call(
        _k_rows_kernel,
        grid=(m // tm,),
        in_specs=[pl.BlockSpec((tm, MIX_WIDTH), lambda i: (i, 1))],
        out_specs=pl.BlockSpec((tm * slots, DIFF_HEAD_DIM), lambda i: (i, 0)),
        out_shape=jax.ShapeDtypeStruct((m * slots, DIFF_HEAD_DIM), F32),
        compiler_params=_params("parallel"),
        name="k_rows",
    )(h)
    return out.reshape(1, batch, slen, DIFF_HEADS, 2, DIFF_HEAD_DIM)


def _v_heads_kernel(x_ref, o_ref):
    for hh in range(DIFF_HEADS):
        o_ref[hh] = x_ref[:, hh * DIFF_V_DIM:(hh + 1) * DIFF_V_DIM]


def _v_heads(h, batch, slen, *, tm):
    nt = slen // tm
    out = pl.pallas_call(
        _v_heads_kernel,
        grid=(batch, nt),
        in_specs=[pl.BlockSpec((tm, MIX_WIDTH), lambda b, i: (b * nt + i, 2))],
        out_specs=pl.BlockSpec((None, DIFF_HEADS, tm, DIFF_V_DIM), lambda b, i: (b, 0, i, 0)),
        out_shape=jax.ShapeDtypeStruct((batch, DIFF_HEADS, slen, DIFF_V_DIM), F32),
        compiler_params=_params("parallel", "parallel"),
        name="v_heads",
    )(h)
    return jnp.transpose(out, (0, 2, 1, 3)).reshape(1, batch, slen, DIFF_HEADS, DIFF_V_DIM)


def _tiles(m, streaming):
    if streaming:
        return dict(mm=dict(tm=m, tn=1024, tk=2048), mm_bf=dict(tm=m, tn=1024, tk=2048),
                    glu=dict(tm=m, tn=1024, tk=1024), o=dict(tm=m, tn=256), ffn_out=dict(tm=m, tn=256),
                    ffn=dict(tm=m, tf=256))
    return dict(mm=dict(tm=2048, tn=1024, tk=1024), mm_bf=dict(tm=2048, tn=1024, tk=2048),
                glu=dict(tm=1024, tn=1024, tk=1024), o=dict(tm=512, tn=256),
                ffn_out=dict(tm=512, tn=256, tight=True),
                ffn=dict(tm=1024, tf=256))


def _run_trunk(x, batch, slen, mem_kv, ssm_state, attn_past, w, wb):
    m = batch * slen
    emit = wb is None
    t = _tiles(m, emit)
    ts = min(slen, 512)
    new_k = new_v = s_re = s_im = None
    copies = {}

    def in_proj(x_in, tensor, li, i):
        tiles = t['mm_bf'] if x_in.dtype == BF16 else t['mm']
        if emit:
            h, copies['in', i] = _matmul(x_in, tensor, li, emit=True, **tiles)
            return h
        return _matmul(x_in, wb['in', i], **tiles)

    def proj_ln(name, a, tensor, x_res, g, b, i):
        if emit:
            xf, xb, copies[name, i] = _matmul_ln(a, tensor, x_res, g, b, i, emit=True, **t[name])
            return xf, xb
        return _matmul_ln(a, wb[name, i], x_res, g, b, i, **t[name])

    x_f32, x_in = x, x
    for i in range(DEPTH):
        li = i // 2
        if i % 2 == 0:
            h = in_proj(x_in, w['w_in_ssm'], li, i)
            if ssm_state is None:
                s0_re = s0_im = jnp.zeros((batch, S5_BLOCKS, 1, S5_HALF), F32)
            else:
                s0_re = ssm_state[0][li].reshape(batch, S5_BLOCKS, 1, S5_HALF)
                s0_im = ssm_state[1][li].reshape(batch, S5_BLOCKS, 1, S5_HALF)
            y, s_re, s_im = _s5(h, w['s5_tables'][li], s0_re, s0_im, batch, slen)
            if emit:
                mix, copies['glu', i] = _glu(y, w['w_glu'], w['b_glu'], li, _act_dtype(ts), emit=True, **t['glu'])
            else:
                mix = _glu(y, wb['glu', i], w['b_glu'], li, _act_dtype(ts), **t['glu'])
            q_col = MIX_WIDTH // MEM_HEAD_DIM
        else:
            h = in_proj(x_in, w['w_in_attn'], li, i)
            lam_init = 0.8 - 0.6 * math.exp(-0.3 * i)
            if attn_past is None:
                mix = _dattn_prompt(h, w, li, lam_init, batch, slen, tq=PROMPT_Q_BLOCK)
            else:
                mix = _dattn_decode(h, attn_past[0], attn_past[1], attn_past[2], w, li, lam_init, batch, slen)
            new_k = _k_rows(h, batch, slen, tm=min(m, 256))
            new_v = _v_heads(h, batch, slen, tm=min(slen, 256))
            q_col = 3 * MIX_WIDTH // MEM_HEAD_DIM
        mem = _mem_attn(h, q_col, mem_kv[i], batch, slen, ts=ts)
        x_f32, x_bf = proj_ln('o', (mix, mem), w['w_o'], x_f32, w['ln1_g'], w['ln1_b'], i)
        if emit:
            hid, copies['ffn_in', i] = _ffn_in(x_bf, w['w_ffn_in'], i, emit=True, **t['ffn'])
        else:
            hid = _ffn_in(x_bf, wb['ffn_in', i], **t['ffn'])
        x_f32, x_in = proj_ln('ffn_out', (hid,), w['w_ffn_out'], x_f32, w['ln2_g'], w['ln2_b'], i)
    return (x_f32, new_k, new_v, s_re, s_im), copies


def kernel(x_prompt, x_sample, cache_attn_k, cache_attn_v, state_ssm_re, state_ssm_im, cache_mem_k, cache_mem_v,
           page_table, mem_prompt, w_in_ssm, ssm_lambda_re, ssm_lambda_im, ssm_log_dt, ssm_b_re, ssm_b_im,
           ssm_c_re, ssm_c_im, ssm_d, w_glu, b_glu, w_in_attn, diff_lambda_q1, diff_lambda_k1, diff_lambda_q2,
           diff_lambda_k2, diff_subln_g, w_mem_kv, w_o, ln1_g, ln1_b, w_ffn_in, w_ffn_out, ln2_g, ln2_b):
    w = dict(w_in_ssm=w_in_ssm, ssm_lambda_re=ssm_lambda_re, ssm_lambda_im=ssm_lambda_im, ssm_log_dt=ssm_log_dt,
             ssm_b_re=ssm_b_re, ssm_b_im=ssm_b_im, ssm_c_re=ssm_c_re, ssm_c_im=ssm_c_im, ssm_d=ssm_d,
             w_glu=w_glu, b_glu=b_glu, w_in_attn=w_in_attn, diff_lambda_q1=diff_lambda_q1,
             diff_lambda_k1=diff_lambda_k1, diff_lambda_q2=diff_lambda_q2, diff_lambda_k2=diff_lambda_k2,
             diff_subln_g=diff_subln_g, w_o=w_o, ln1_g=ln1_g, ln1_b=ln1_b, w_ffn_in=w_ffn_in,
             w_ffn_out=w_ffn_out, ln2_g=ln2_g, ln2_b=ln2_b)
    bp, sp, _ = x_prompt.shape
    bs, ss, _ = x_sample.shape

    mem_rows = mem_prompt.reshape(bp * MEM_TOKENS, D_MODEL)
    mem_kv_p = [_matmul(mem_rows, w_mem_kv, l, tm=bp * MEM_TOKENS, tn=1024, tk=1024) for l in range(DEPTH)]
    mem_k_prompt = jnp.stack([kv[:, :MEM_WIDTH].reshape(bp, MEM_TOKENS, MEM_HEADS, MEM_HEAD_DIM) for kv in mem_kv_p])
    mem_v_prompt = jnp.stack([kv[:, MEM_WIDTH:].reshape(bp, MEM_TOKENS, MEM_HEADS, MEM_HEAD_DIM) for kv in mem_kv_p])
    mem_kv_s = [jnp.concatenate([cache_mem_k[l].reshape(bs * MEM_TOKENS, MEM_WIDTH),
                                 cache_mem_v[l].reshape(bs * MEM_TOKENS, MEM_WIDTH)], axis=1) for l in range(DEPTH)]

    w['s5_tables'] = [_s5_tables(ssm_lambda_re[l], ssm_lambda_im[l], ssm_log_dt[l], ssm_b_re[l], ssm_b_im[l],
                                 ssm_c_re[l], ssm_c_im[l], ssm_d[l]) for l in range(ssm_d.shape[0])]

    (ys, ks, vs, srs, sis), wb = _run_trunk(x_sample.reshape(bs * ss, D_MODEL), bs, ss, mem_kv_s,
                                            (state_ssm_re, state_ssm_im),
                                            (cache_attn_k, cache_attn_v, page_table), w, None)
    (yp, kp, vp, srp, sip), _ = _run_trunk(x_prompt.reshape(bp * sp, D_MODEL), bp, sp, mem_kv_p, None, None, w, wb)

    def sshape(s, b):
        return s.reshape(1, b, SSM_GROUPS, SSM_STATE)

    return (yp.reshape(bp, sp, D_MODEL), ys.reshape(bs, ss, D_MODEL), kp, vp, ks, vs,
            sshape(srp, bp), sshape(sip, bp), sshape(srs, bs), sshape(sis, bs),
            mem_k_prompt, mem_v_prompt)
```

```python
import functools
import math

import jax
import jax.numpy as jnp
from jax import lax
from jax.experimental import pallas as pl
from jax.experimental.pallas import tpu as pltpu

F32 = jnp.float32
BF16 = jnp.bfloat16

D_MODEL = 4096
DEPTH = 2
MEM_TOKENS = 256
MEM_HEADS = 4
MEM_HEAD_DIM = 256
MEM_WIDTH = MEM_HEADS * MEM_HEAD_DIM
MIX_WIDTH = D_MODEL - MEM_WIDTH
SSM_GROUP = 16
SSM_GROUPS = MIX_WIDTH // SSM_GROUP
SSM_STATE = 64
DIFF_HEAD_DIM = 128
DIFF_HEADS = MIX_WIDTH // (2 * DIFF_HEAD_DIM)
DIFF_V_DIM = 2 * DIFF_HEAD_DIM
D_FF = 11008
ALPHA = (2 * DEPTH) ** 0.25
LN_EPS = 1e-5
PAGE_SIZE = 128

VMEM_LIMIT_BYTES = 58 * 1024 * 1024
LANES = 128
SUBLANES = 8

S5_GROUPS_PER_BLOCK = 16
S5_CH = S5_GROUPS_PER_BLOCK * SSM_GROUP
S5_HALF = S5_GROUPS_PER_BLOCK * SSM_STATE
S5_BLOCKS = SSM_GROUPS // S5_GROUPS_PER_BLOCK
S5_SEG = 64
S5_PITCH = S5_SEG + SUBLANES
LN_ROWS = 128
DECODE_PAGES = 4
PROMPT_Q_BLOCK = 512


def _params(*sem):
    return pltpu.CompilerParams(dimension_semantics=sem, vmem_limit_bytes=VMEM_LIMIT_BYTES)


def _dot(a, b):
    return jnp.dot(a, b, preferred_element_type=F32)


def _dot_nt(a, b):
    return lax.dot_general(a, b, (((1,), (1,)), ((), ())), preferred_element_type=F32)


def _w_spec(w, layer, block, index):
    if w.ndim == 3:
        return pl.BlockSpec((None,) + block, lambda *g: (layer,) + index(*g))
    return pl.BlockSpec(block, index)


def _accumulate(ref, first, term):
    @pl.when(first)
    def _():
        ref[...] = term()

    @pl.when(jnp.logical_not(first))
    def _():
        ref[...] += term()


def _mm_kernel(x_ref, w_ref, o_ref, *wb_ref):
    w = w_ref[...].astype(BF16)
    if wb_ref:
        wb_ref[0][...] = w
    _accumulate(o_ref, pl.program_id(2) == 0, lambda: _dot(x_ref[...].astype(BF16), w))


def _matmul(x, w, layer=None, *, tm, tn, tk, emit=False):
    m, k = x.shape
    n = w.shape[-1]
    assert not emit or m == tm
    out_specs = [pl.BlockSpec((tm, tn), lambda i, j, kk: (i, j))]
    out_shape = [jax.ShapeDtypeStruct((m, n), F32)]
    if emit:
        out_specs.append(pl.BlockSpec((tk, tn), lambda i, j, kk: (kk, j)))
        out_shape.append(jax.ShapeDtypeStruct((k, n), BF16))
    res = pl.pallas_call(
        _mm_kernel,
        grid=(m // tm, n // tn, k // tk),
        in_specs=[
            pl.BlockSpec((tm, tk), lambda i, j, kk: (i, kk)),
            _w_spec(w, layer, (tk, tn), lambda i, j, kk: (kk, j)),
        ],
        out_specs=out_specs,
        out_shape=out_shape,
        compiler_params=_params("parallel", "parallel", "arbitrary"),
        name="matmul",
    )(x, w)
    return res if emit else res[0]


def _glu_kernel(yk_ref, w_ref, b_ref, yn_ref, o_ref, *rest):
    acc_ref = rest[-1]
    kk = pl.program_id(2)
    w = w_ref[...].astype(BF16)
    if len(rest) == 2:
        rest[0][...] = w
    _accumulate(acc_ref, kk == 0, lambda: _dot(yk_ref[...].astype(BF16), w))

    @pl.when(kk == pl.num_programs(2) - 1)
    def _():
        gate = jax.nn.sigmoid(acc_ref[...] + b_ref[...])
        o_ref[...] = (yn_ref[...] * gate).astype(o_ref.dtype)


def _act_dtype(rows_per_block):
    return BF16 if rows_per_block % (2 * SUBLANES) == 0 else F32


def _glu(y, w, b, layer, out_dtype, *, tm, tn, tk, emit=False):
    m, k = y.shape
    n = w.shape[-1]
    assert not emit or m == tm
    out_specs = [pl.BlockSpec((tm, tn), lambda i, j, kk: (i, j))]
    out_shape = [jax.ShapeDtypeStruct((m, n), out_dtype)]
    if emit:
        out_specs.append(pl.BlockSpec((tk, tn), lambda i, j, kk: (kk, j)))
        out_shape.append(jax.ShapeDtypeStruct((k, n), BF16))
    res = pl.pallas_call(
        _glu_kernel,
        grid=(m // tm, n // tn, k // tk),
        in_specs=[
            pl.BlockSpec((tm, tk), lambda i, j, kk: (i, kk)),
            _w_spec(w, layer, (tk, tn), lambda i, j, kk: (kk, j)),
            pl.BlockSpec((None, 1, tn), lambda i, j, kk: (layer, 0, j)),
            pl.BlockSpec((tm, tn), lambda i, j, kk: (i, j)),
        ],
        out_specs=out_specs,
        out_shape=out_shape,
        scratch_shapes=[pltpu.VMEM((tm, tn), F32)],
        compiler_params=_params("parallel", "parallel", "arbitrary"),
        name="glu",
    )(y, w, b.reshape(b.shape[0], 1, b.shape[1]), y)
    return res if emit else res[0]


def _ln_epilogue(result, xs, g_ref, b_ref, of_ref, ob_ref):
    panels, tm, tn = xs.shape
    rc = min(tm, LN_ROWS)
    width = panels * tn

    def rows(c, _):
        r = pl.ds(pl.multiple_of(c * rc, rc), rc)
        cols = [slice(p * tn, (p + 1) * tn) for p in range(panels)]
        ys = [ALPHA * xs[p, r, :] + result(p, r, cs) for p, cs in enumerate(cols)]
        mu = sum(jnp.sum(y, axis=-1, keepdims=True) for y in ys) / width
        ds = [y - mu for y in ys]
        var = sum(jnp.sum(d * d, axis=-1, keepdims=True) for d in ds) / width
        inv = lax.rsqrt(var + LN_EPS)
        for d, cs in zip(ds, cols):
            out = d * inv * g_ref[:, cs] + b_ref[:, cs]
            of_ref[r, cs] = out
            ob_ref[r, cs] = out.astype(BF16)
        return 0

    lax.fori_loop(0, tm // rc, rows, 0)


def _k_segments(a_widths, w_heights):
    segs, ia, iw, oa, ow = [], 0, 0, 0, 0
    while ia < len(a_widths):
        length = min(a_widths[ia] - oa, w_heights[iw] - ow)
        segs.append((ia, oa, iw, ow, length))
        oa += length
        ow += length
        if oa == a_widths[ia]:
            ia, oa = ia + 1, 0
        if ow == w_heights[iw]:
            iw, ow = iw + 1, 0
    return segs


def _mm_ln_kernel(*refs, n_a, n_w):
    a_refs = refs[:n_a]
    w_refs = refs[n_a:n_a + n_w]
    xp_ref, g_ref, b_ref, of_ref, ob_ref, *rest = refs[n_a + n_w:]
    acc, xs = rest[-2:]
    j = pl.program_id(1)
    ws = [w_ref[...].astype(BF16) for w_ref in w_refs]
    if len(rest) == 3:
        rest[0][...] = ws[0]
    total = None
    for ia, oa, iw, ow, length in _k_segments([a.shape[1] for a in a_refs], [w.shape[0] for w in ws]):
        term = _dot(a_refs[ia][:, oa:oa + length].astype(BF16), ws[iw][ow:ow + length])
        total = term if total is None else total + term
    acc[j] = total
    xs[j] = xp_ref[...]

    @pl.when(j == pl.num_programs(1) - 1)
    def _():
        _ln_epilogue(lambda p, r, cs: acc[p, r, :], xs, g_ref, b_ref, of_ref, ob_ref)


def _panel_spec(w, layer, k, tn, piece=0, pieces=1):
    if w.dtype == BF16:
        assert w.shape[1:] == (k, tn)
        return pl.BlockSpec((None, k // pieces, tn), lambda i, j: (j, piece, 0))
    return pl.BlockSpec((None, k // pieces, tn), lambda i, j: (layer, piece, j))


def _matmul_ln(a_parts, w, x, g, b, layer, *, tm, tn, emit=False, tight=False, w_pieces=1):
    m, n = x.shape
    k = sum(a.shape[1] for a in a_parts)
    assert not emit or (m == tm and w_pieces == 1)
    mode = pl.Buffered(1) if tight else None
    out_specs = [pl.BlockSpec((tm, n), lambda i, j: (i, 0), pipeline_mode=mode),
                 pl.BlockSpec((tm, n), lambda i, j: (i, 0))]
    out_shape = [jax.ShapeDtypeStruct((m, n), F32), jax.ShapeDtypeStruct((m, n), BF16)]
    if emit:
        out_specs.append(pl.BlockSpec((None, k, tn), lambda i, j: (j, 0, 0)))
        out_shape.append(jax.ShapeDtypeStruct((n // tn, k, tn), BF16))
    return pl.pallas_call(
        functools.partial(_mm_ln_kernel, n_a=len(a_parts), n_w=w_pieces),
        grid=(m // tm, n // tn),
        in_specs=[
            *[pl.BlockSpec((tm, a.shape[1]), lambda i, j: (i, 0), pipeline_mode=mode) for a in a_parts],
            *[_panel_spec(w, layer, k, tn, q, w_pieces) for q in range(w_pieces)],
            pl.BlockSpec((tm, tn), lambda i, j: (i, j)),
            pl.BlockSpec((None, 1, n), lambda i, j: (layer, 0, 0)),
            pl.BlockSpec((None, 1, n), lambda i, j: (layer, 0, 0)),
        ],
        out_specs=out_specs,
        out_shape=out_shape,
        scratch_shapes=[pltpu.VMEM((n // tn, tm, tn), F32), pltpu.VMEM((n // tn, tm, tn), F32)],
        compiler_params=_params("parallel", "arbitrary"),
        name="matmul_ln",
    )(*a_parts, *([w] * w_pieces), x, g.reshape(g.shape[0], 1, n), b.reshape(b.shape[0], 1, n))


def _ffn_in_kernel(x_ref, wg_ref, wu_ref, o_ref, *wb_refs):
    x = x_ref[...]
    wg = wg_ref[...].astype(BF16)
    wu = wu_ref[...].astype(BF16)
    if wb_refs:
        wb_refs[0][...] = wg
        wb_refs[1][...] = wu
    o_ref[...] = (jax.nn.silu(_dot(x, wg)) * _dot(x, wu)).astype(o_ref.dtype)


def _ffn_in(x, w, layer=None, *, tm, tf, emit=False):
    m, k = x.shape
    nf = D_FF // tf
    assert not emit or m == tm
    if isinstance(w, tuple):
        wg, wu = w
        w_specs = [_panel_spec(wg, None, k, tf), _panel_spec(wu, None, k, tf)]
    else:
        wg = wu = w
        w_specs = [pl.BlockSpec((None, k, tf), lambda i, j: (layer, 0, j)),
                   pl.BlockSpec((None, k, tf), lambda i, j: (layer, 0, j + nf))]
    out_specs = [pl.BlockSpec((tm, tf), lambda i, j: (i, j))]
    out_shape = [jax.ShapeDtypeStruct((m, D_FF), BF16)]
    if emit:
        out_specs += [pl.BlockSpec((None, k, tf), lambda i, j: (j, 0, 0))] * 2
        out_shape += [jax.ShapeDtypeStruct((nf, k, tf), BF16)] * 2
    res = pl.pallas_call(
        _ffn_in_kernel,
        grid=(m // tm, nf),
        in_specs=[pl.BlockSpec((tm, k), lambda i, j: (i, 0), pipeline_mode=pl.Buffered(1))] + w_specs,
        out_specs=out_specs,
        out_shape=out_shape,
        compiler_params=_params("parallel", "arbitrary"),
        name="ffn_in",
    )(x, wg, wu)
    return (res[0], (res[1], res[2])) if emit else res[0]


def _s5_kernel(u_ref, bb_ref, cb_ref, are_ref, aim_ref, apre_ref, apim_ref, d_ref, s0re_ref, s0im_ref,
               y_ref, sre_ref, sim_ref,
               uslab, uperm, bu, yslab, cre, cim, *, seg, pitch):
    half = S5_HALF

    @pl.when(pl.program_id(2) == 0)
    def _():
        cre[...] = s0re_ref[...]
        cim[...] = s0im_ref[...]

    if seg == 1:
        uperm[...] = u_ref[...]
    else:
        for i in range(SUBLANES):
            for j in range(S5_CH // LANES):
                uslab[j, i * pitch:i * pitch + seg, :] = u_ref[i * seg:(i + 1) * seg, j * LANES:(j + 1) * LANES]

        def gather(r, _):
            row = pl.ds(pl.multiple_of(r * SUBLANES, SUBLANES), SUBLANES)
            for j in range(S5_CH // LANES):
                uperm[row, j * LANES:(j + 1) * LANES] = uslab[j, pl.ds(r, SUBLANES, stride=pitch), :]
            return 0

        lax.fori_loop(0, seg, gather, 0, unroll=min(seg, 4))

    up = uperm[...]
    bu[...] = _dot(up.astype(BF16), bb_ref[...])

    are = jnp.broadcast_to(are_ref[...], (SUBLANES, half))
    aim = jnp.broadcast_to(aim_ref[...], (SUBLANES, half))

    def advance(r, sre, sim):
        row = pl.ds(pl.multiple_of(r * SUBLANES, SUBLANES), SUBLANES)
        nre = are * sre - aim * sim + bu[row, :half]
        nim = are * sim + aim * sre + bu[row, half:]
        return row, nre, nim

    def local_step(r, carry):
        _, nre, nim = advance(r, *carry)
        return nre, nim

    zero = jnp.zeros((SUBLANES, half), F32)
    ere, eim = lax.fori_loop(0, seg, local_step, (zero, zero), unroll=min(seg, 2))

    apre = apre_ref[...]
    apim = apim_ref[...]
    rows_re = [cre[...]]
    rows_im = [cim[...]]
    for i in range(1, SUBLANES + 1):
        pre, pim = rows_re[-1], rows_im[-1]
        rows_re.append(ere[i - 1:i] + apre * pre - apim * pim)
        rows_im.append(eim[i - 1:i] + apre * pim + apim * pre)
    cre[...] = rows_re[-1]
    cim[...] = rows_im[-1]
    sre_ref[...] = rows_re[-1]
    sim_ref[...] = rows_im[-1]
    cin_re = jnp.concatenate(rows_re[:SUBLANES], axis=0)
    cin_im = jnp.concatenate(rows_im[:SUBLANES], axis=0)

    def true_step(r, carry):
        row, nre, nim = advance(r, *carry)
        bu[row, :half] = nre
        bu[row, half:] = nim
        return nre, nim

    lax.fori_loop(0, seg, true_step, (cin_re, cin_im), unroll=min(seg, 2))

    yp = _dot(bu[...].astype(BF16), cb_ref[...]) + d_ref[...] * up
    yp = jax.nn.gelu(yp)
    if seg == 1:
        y_ref[...] = yp
    else:
        for j in range(S5_CH // LANES):
            yslab[j] = yp[:, j * LANES:(j + 1) * LANES]

        def scatter(q, _):
            for i in range(SUBLANES):
                dst = pl.ds(pl.multiple_of(i * seg + q * SUBLANES, SUBLANES), SUBLANES)
                src = pl.ds(q * SUBLANES * SUBLANES + i, SUBLANES, stride=SUBLANES)
                for j in range(S5_CH // LANES):
                    y_ref[dst, j * LANES:(j + 1) * LANES] = yslab[j, src, :]
            return 0

        lax.fori_loop(0, seg // SUBLANES, scatter, 0)


def _s5_tables(lam_re, lam_im, log_dt, b_re, b_im, c_re, c_im, d_skip):
    dt = jnp.exp(log_dt)[:, None]
    mag = jnp.exp(lam_re * dt)
    ab_re = mag * jnp.cos(lam_im * dt)
    ab_im = mag * jnp.sin(lam_im * dt)
    den = lam_re * lam_re + lam_im * lam_im
    nr = ab_re - 1.0
    f_re = (nr * lam_re + ab_im * lam_im) / den
    f_im = (ab_im * lam_re - nr * lam_im) / den
    bb_re = f_re[..., None] * b_re - f_im[..., None] * b_im
    bb_im = f_re[..., None] * b_im + f_im[..., None] * b_re
    eye = jnp.eye(S5_GROUPS_PER_BLOCK, dtype=F32)
    gpb = S5_GROUPS_PER_BLOCK

    def expand_b(bb):
        bb = bb.reshape(S5_BLOCKS, gpb, SSM_STATE, SSM_GROUP)
        return jnp.einsum('bgpc,gh->bgchp', bb, eye).reshape(S5_BLOCKS, S5_CH, S5_HALF)

    def expand_c(cc):
        cc = cc.reshape(S5_BLOCKS, gpb, SSM_GROUP, SSM_STATE)
        return jnp.einsum('bgcp,gh->bgphc', cc, eye).reshape(S5_BLOCKS, S5_HALF, S5_CH)

    bblk = jnp.concatenate([expand_b(bb_re), expand_b(bb_im)], axis=2).astype(BF16)
    cblk = jnp.concatenate([expand_c(c_re), -expand_c(c_im)], axis=1).astype(BF16)
    vec = lambda v: v.reshape(S5_BLOCKS, 1, S5_HALF)
    return bblk, cblk, vec(ab_re), vec(ab_im), d_skip.reshape(S5_BLOCKS, 1, S5_CH)


def _s5(h, tables, s0_re, s0_im, batch, slen):
    bblk, cblk, are, aim, dsk = tables
    seg = S5_SEG if slen >= SUBLANES * S5_SEG else slen // SUBLANES
    apre, apim = are, aim
    for _ in range(int(math.log2(seg))):
        apre, apim = apre * apre - apim * apim, 2.0 * apre * apim
    tc = SUBLANES * seg
    nchunk = slen // tc
    pitch = S5_PITCH
    vec = pl.BlockSpec((None, 1, S5_HALF), lambda g, b, c: (g, 0, 0))
    state = pl.BlockSpec((None, None, 1, S5_HALF), lambda g, b, c: (b, g, 0, 0))
    y, sre, sim = pl.pallas_call(
        functools.partial(_s5_kernel, seg=seg, pitch=pitch),
        grid=(S5_BLOCKS, batch, nchunk),
        in_specs=[
            pl.BlockSpec((tc, S5_CH), lambda g, b, c: (b * nchunk + c, g)),
            pl.BlockSpec((None, S5_CH, 2 * S5_HALF), lambda g, b, c: (g, 0, 0)),
            pl.BlockSpec((None, 2 * S5_HALF, S5_CH), lambda g, b, c: (g, 0, 0)),
            vec, vec, vec, vec,
            pl.BlockSpec((None, 1, S5_CH), lambda g, b, c: (g, 0, 0)),
            state, state,
        ],
        out_specs=[
            pl.BlockSpec((tc, S5_CH), lambda g, b, c: (b * nchunk + c, g)),
            state, state,
        ],
        out_shape=[
            jax.ShapeDtypeStruct((batch * slen, MIX_WIDTH), F32),
            jax.ShapeDtypeStruct((batch, S5_BLOCKS, 1, S5_HALF), F32),
            jax.ShapeDtypeStruct((batch, S5_BLOCKS, 1, S5_HALF), F32),
        ],
        scratch_shapes=[
            pltpu.VMEM((S5_CH // LANES, SUBLANES * pitch, LANES), F32),
            pltpu.VMEM((tc, S5_CH), F32),
            pltpu.VMEM((tc, 2 * S5_HALF), F32),
            pltpu.VMEM((S5_CH // LANES, tc, LANES), F32),
            pltpu.VMEM((1, S5_HALF), F32),
            pltpu.VMEM((1, S5_HALF), F32),
        ],
        compiler_params=_params("parallel", "parallel", "arbitrary"),
        name="s5_mixer",
    )(h, bblk, cblk, are, aim, apre, apim, dsk, s0_re, s0_im)
    return y, sre, sim


def _diff_lambda(lq1, lk1, lq2, lk2, lam_init):
    e1 = jnp.exp(jnp.sum(lq1[...] * lk1[...], axis=-1, keepdims=True))
    e2 = jnp.exp(jnp.sum(lq2[...] * lk2[...], axis=-1, keepdims=True))
    return e1 - e2 + lam_init


def _softmax_update(s, v, m, l, acc):
    m_new = jnp.maximum(m, jnp.max(s, axis=-1, keepdims=True))
    alpha = jnp.exp(m - m_new)
    p = jnp.exp(s - m_new)
    l_new = alpha * l + jnp.sum(p, axis=-1, keepdims=True)
    acc_new = alpha * acc + _dot(p.astype(BF16), v)
    return m_new, l_new, acc_new


def _head_norm(o, g, lam_init):
    o = o * lax.rsqrt(jnp.mean(o * o, axis=-1, keepdims=True) + LN_EPS) * g
    return o * (1.0 - lam_init)


def _dattn_prompt_kernel(lq1, lk1, lq2, lk2, g_ref, q_ref, k_ref, v_ref, o_ref, k1s, k2s, vs, *, tq, lam_init):
    qi = pl.program_id(2)
    d = DIFF_HEAD_DIM
    scale = d ** -0.5
    update = _softmax_update

    @pl.when(qi == 0)
    def _():
        k1s[...] = k_ref[:, :d].astype(BF16)
        k2s[...] = k_ref[:, d:].astype(BF16)
        vs[...] = v_ref[...].astype(BF16)

    lam = _diff_lambda(lq1, lk1, lq2, lk2, lam_init)
    q1 = q_ref[:, :d].astype(BF16)
    q2 = q_ref[:, d:].astype(BF16)

    def block(j):
        rows = pl.ds(pl.multiple_of(j * tq, tq), tq)
        return _dot_nt(q1, k1s[rows, :]) * scale, _dot_nt(q2, k2s[rows, :]) * scale, vs[rows, :]

    def body(j, st):
        s1, s2, v = block(j)
        return update(s1, v, *st[:3]) + update(s2, v, *st[3:])

    init = (jnp.full((tq, 1), -jnp.inf, F32), jnp.zeros((tq, 1), F32), jnp.zeros((tq, DIFF_V_DIM), F32))
    st = lax.fori_loop(0, qi, body, init + init)

    s1, s2, v = block(qi)
    causal = lax.broadcasted_iota(jnp.int32, (tq, tq), 0) >= lax.broadcasted_iota(jnp.int32, (tq, tq), 1)
    s1 = jnp.where(causal, s1, -jnp.inf)
    s2 = jnp.where(causal, s2, -jnp.inf)
    _, l1, a1 = update(s1, v, *st[:3])
    _, l2, a2 = update(s2, v, *st[3:])
    o = a1 / l1 - lam * (a2 / l2)
    o_ref[...] = _head_norm(o, g_ref[...], lam_init).astype(o_ref.dtype)


def _row3(v):
    return v.reshape(v.shape[0], 1, v.shape[1])


def _dattn_prompt(h, w, layer, lam_init, batch, slen, *, tq):
    nq = slen // tq
    lam_spec = pl.BlockSpec((None, 1, DIFF_HEAD_DIM), lambda b, hh, qi: (layer, 0, 0))
    row3 = _row3
    return pl.pallas_call(
        functools.partial(_dattn_prompt_kernel, tq=tq, lam_init=lam_init),
        grid=(batch, DIFF_HEADS, nq),
        in_specs=[
            lam_spec, lam_spec, lam_spec, lam_spec,
            pl.BlockSpec((None, 1, DIFF_V_DIM), lambda b, hh, qi: (layer, 0, 0)),
            pl.BlockSpec((tq, DIFF_V_DIM), lambda b, hh, qi: (b * nq + qi, hh)),
            pl.BlockSpec((slen, DIFF_V_DIM), lambda b, hh, qi: (b, DIFF_HEADS + hh)),
            pl.BlockSpec((slen, DIFF_V_DIM), lambda b, hh, qi: (b, 2 * DIFF_HEADS + hh)),
        ],
        out_specs=pl.BlockSpec((tq, DIFF_V_DIM), lambda b, hh, qi: (b * nq + qi, hh)),
        out_shape=jax.ShapeDtypeStruct((batch * slen, MIX_WIDTH), BF16),
        scratch_shapes=[
            pltpu.VMEM((slen, DIFF_HEAD_DIM), BF16),
            pltpu.VMEM((slen, DIFF_HEAD_DIM), BF16),
            pltpu.VMEM((slen, DIFF_V_DIM), BF16),
        ],
        compiler_params=_params("parallel", "parallel", "arbitrary"),
        name="diff_attn_prompt",
    )(row3(w['diff_lambda_q1']), row3(w['diff_lambda_k1']), row3(w['diff_lambda_q2']), row3(w['diff_lambda_k2']),
      row3(w['diff_subln_g']), h, h, h)


def _dattn_decode_kernel(pt_ref, lq1, lk1, lq2, lk2, g_ref, q_ref, kn_ref, vn_ref, *rest, tnew, lam_init):
    del pt_ref
    kc_refs = rest[:DECODE_PAGES]
    vc_refs = rest[DECODE_PAGES:2 * DECODE_PAGES]
    o_ref, m_s, l_s, acc_s, q_s = rest[2 * DECODE_PAGES:]
    p = pl.program_id(1)
    d = DIFF_HEAD_DIM
    scale = d ** -0.5
    kv_rows = DIFF_HEADS * 2
    hrows = 2 * tnew

    @pl.when(p == 0)
    def _():
        m_s[...] = jnp.full_like(m_s, -jnp.inf)
        l_s[...] = jnp.zeros_like(l_s)
        acc_s[...] = jnp.zeros_like(acc_s)
        q_s[...] = q_ref[...].astype(BF16)

    def q_of(j):
        return q_s[:, j * d:(j + 1) * d]

    def update(s, pv_of):
        m_old = m_s[...]
        m_new = jnp.maximum(m_old, jnp.max(s, axis=-1, keepdims=True))
        alpha = jnp.exp(m_old - m_new)
        prob = jnp.exp(s - m_new)
        l_s[...] = alpha * l_s[...] + jnp.sum(prob, axis=-1, keepdims=True)
        m_s[...] = m_new
        pb = prob.astype(BF16)
        pv = jnp.concatenate([pv_of(hh, pb[hh * hrows:(hh + 1) * hrows]) for hh in range(DIFF_HEADS)], axis=0)
        acc_s[...] = alpha * acc_s[...] + pv

    s_pages = []
    for kc_ref in kc_refs:
        s_pages.append(jnp.concatenate(
            [_dot_nt(q_of(j), kc_ref[pl.ds(j, PAGE_SIZE, stride=kv_rows), :].astype(BF16)) for j in range(kv_rows)],
            axis=0))
    s_all = jnp.concatenate(s_pages, axis=1) * scale

    def pv_pages(hh, pb):
        out = None
        for g, vc_ref in enumerate(vc_refs):
            term = _dot(pb[:, g * PAGE_SIZE:(g + 1) * PAGE_SIZE], vc_ref[hh].astype(BF16))
            out = term if out is None else out + term
        return out

    update(s_all, pv_pages)

    @pl.when(p == pl.num_programs(1) - 1)
    def _():
        lam = _diff_lambda(lq1, lk1, lq2, lk2, lam_init)
        pad_rows = jnp.zeros((LANES - tnew, d), BF16)
        pad_v = jnp.zeros((LANES - tnew, DIFF_V_DIM), BF16)
        row = lax.broadcasted_iota(jnp.int32, (tnew, LANES), 0)
        col = lax.broadcasted_iota(jnp.int32, (tnew, LANES), 1)
        causal = jnp.concatenate([col <= row] * kv_rows, axis=0)
        s_new = jnp.concatenate(
            [_dot_nt(q_of(j), jnp.concatenate([kn_ref[:, j * d:(j + 1) * d].astype(BF16), pad_rows], axis=0))
             for j in range(kv_rows)], axis=0) * scale

        def pv_new(hh, pb):
            v = jnp.concatenate([vn_ref[:, hh * DIFF_V_DIM:(hh + 1) * DIFF_V_DIM].astype(BF16), pad_v], axis=0)
            return _dot(pb, v)

        update(jnp.where(causal, s_new, -jnp.inf), pv_new)
        acc = acc_s[...] / l_s[...]
        for hh in range(DIFF_HEADS):
            o = acc[hh * hrows:hh * hrows + tnew] - lam * acc[hh * hrows + tnew:(hh + 1) * hrows]
            o_ref[:, hh * DIFF_V_DIM:(hh + 1) * DIFF_V_DIM] = _head_norm(o, g_ref[...], lam_init).astype(o_ref.dtype)


def _dattn_decode(h, k_pool, v_pool, page_table, w, layer, lam_init, batch, tnew):
    n_pool = k_pool.shape[1]
    n_pages = page_table.shape[1]
    kv_rows = DIFF_HEADS * 2
    kc = k_pool[layer].reshape(n_pool * PAGE_SIZE * kv_rows, DIFF_HEAD_DIM)
    vc = jnp.transpose(v_pool[layer], (0, 2, 1, 3))
    lam_spec = pl.BlockSpec((None, 1, DIFF_HEAD_DIM), lambda b, p, pt: (layer, 0, 0))
    row3 = _row3
    new_spec = lambda blk: pl.BlockSpec((tnew, MIX_WIDTH), lambda b, p, pt: (b, blk))
    k_spec = lambda g: pl.BlockSpec((PAGE_SIZE * kv_rows, DIFF_HEAD_DIM),
                                    lambda b, p, pt: (pt[b, p * DECODE_PAGES + g], 0))
    v_spec = lambda g: pl.BlockSpec((None, DIFF_HEADS, PAGE_SIZE, DIFF_V_DIM),
                                    lambda b, p, pt: (pt[b, p * DECODE_PAGES + g], 0, 0, 0))
    state_rows = kv_rows * tnew
    grid_spec = pltpu.PrefetchScalarGridSpec(
        num_scalar_prefetch=1,
        grid=(batch, n_pages // DECODE_PAGES),
        in_specs=[
            lam_spec, lam_spec, lam_spec, lam_spec,
            pl.BlockSpec((None, 1, DIFF_V_DIM), lambda b, p, pt: (layer, 0, 0)),
            new_spec(0), new_spec(1), new_spec(2),
            *[k_spec(g) for g in range(DECODE_PAGES)],
            *[v_spec(g) for g in range(DECODE_PAGES)],
        ],
        out_specs=pl.BlockSpec((tnew, MIX_WIDTH), lambda b, p, pt: (b, 0)),
        scratch_shapes=[
            pltpu.VMEM((state_rows, 1), F32),
            pltpu.VMEM((state_rows, 1), F32),
            pltpu.VMEM((state_rows, DIFF_V_DIM), F32),
            pltpu.VMEM((tnew, MIX_WIDTH), BF16),
        ],
    )
    return pl.pallas_call(
        functools.partial(_dattn_decode_kernel, tnew=tnew, lam_init=lam_init),
        grid_spec=grid_spec,
        out_shape=jax.ShapeDtypeStruct((batch * tnew, MIX_WIDTH), _act_dtype(tnew)),
        compiler_params=_params("parallel", "arbitrary"),
        name="diff_attn_decode",
    )(page_table, row3(w['diff_lambda_q1']), row3(w['diff_lambda_k1']), row3(w['diff_lambda_q2']),
      row3(w['diff_lambda_k2']), row3(w['diff_subln_g']), h, h, h, *([kc] * DECODE_PAGES), *([vc] * DECODE_PAGES))


def _mem_attn_kernel(q_ref, mk_ref, mv_ref, o_ref):
    s = _dot_nt(q_ref[...].astype(BF16), mk_ref[...].astype(BF16)) * (MEM_HEAD_DIM ** -0.5)
    p = jnp.exp(s - jnp.max(s, axis=-1, keepdims=True))
    l = jnp.sum(p, axis=-1, keepdims=True)
    o_ref[...] = (_dot(p.astype(BF16), mv_ref[...].astype(BF16)) / l).astype(o_ref.dtype)


def _mem_attn(h, q_col, mem_kv, batch, slen, *, ts):
    ns = slen // ts
    return pl.pallas_call(
        _mem_attn_kernel,
        grid=(batch, ns, MEM_HEADS),
        in_specs=[
            pl.BlockSpec((ts, MEM_HEAD_DIM), lambda b, si, hh: (b * ns + si, q_col + hh)),
            pl.BlockSpec((MEM_TOKENS, MEM_HEAD_DIM), lambda b, si, hh: (b, hh)),
            pl.BlockSpec((MEM_TOKENS, MEM_HEAD_DIM), lambda b, si, hh: (b, MEM_HEADS + hh)),
        ],
        out_specs=pl.BlockSpec((ts, MEM_HEAD_DIM), lambda b, si, hh: (b * ns + si, hh)),
        out_shape=jax.ShapeDtypeStruct((batch * slen, MEM_WIDTH), _act_dtype(ts)),
        compiler_params=_params("parallel", "parallel", "parallel"),
        name="mem_attn",
    )(h, mem_kv, mem_kv)


def _k_rows_kernel(x_ref, o_ref):
    tm = x_ref.shape[0]
    slots = x_ref.shape[1] // DIFF_HEAD_DIM
    for j in range(slots):
        o_ref[pl.ds(j, tm, stride=slots), :] = x_ref[:, j * DIFF_HEAD_DIM:(j + 1) * DIFF_HEAD_DIM]


def _k_rows(h, batch, slen, *, tm):
    m = h.shape[0]
    slots = MIX_WIDTH // DIFF_HEAD_DIM
    out = pl.pallas_call(
        _k_rows_kernel,
        grid=(m // tm,),
        in_specs=[pl.BlockSpec((tm, MIX_WIDTH), lambda i: (i, 1))],
        out_specs=pl.BlockSpec((tm * slots, DIFF_HEAD_DIM), lambda i: (i, 0)),
        out_shape=jax.ShapeDtypeStruct((m * slots, DIFF_HEAD_DIM), F32),
        compiler_params=_params("parallel"),
        name="k_rows",
    )(h)
    return out.reshape(1, batch, slen, DIFF_HEADS, 2, DIFF_HEAD_DIM)


def _v_heads_kernel(x_ref, o_ref):
    for hh in range(DIFF_HEADS):
        o_ref[hh] = x_ref[:, hh * DIFF_V_DIM:(hh + 1) * DIFF_V_DIM]


def _v_heads(h, batch, slen, *, tm):
    nt = slen // tm
    out = pl.pallas_call(
        _v_heads_kernel,
        grid=(batch, nt),
        in_specs=[pl.BlockSpec((tm, MIX_WIDTH), lambda b, i: (b * nt + i, 2))],
        out_specs=pl.BlockSpec((None, DIFF_HEADS, tm, DIFF_V_DIM), lambda b, i: (b, 0, i, 0)),
        out_shape=jax.ShapeDtypeStruct((batch, DIFF_HEADS, slen, DIFF_V_DIM), F32),
        compiler_params=_params("parallel", "parallel"),
        name="v_heads",
    )(h)
    return jnp.transpose(out, (0, 2, 1, 3)).reshape(1, batch, slen, DIFF_HEADS, DIFF_V_DIM)


def _tiles(m, streaming):
    if streaming:
        return dict(mm=dict(tm=m, tn=1024, tk=2048), mm_bf=dict(tm=m, tn=1024, tk=2048),
                    glu=dict(tm=m, tn=1024, tk=1024), o=dict(tm=m, tn=256), ffn_out=dict(tm=m, tn=256),
                    ffn=dict(tm=m, tf=256))
    return dict(mm=dict(tm=2048, tn=1024, tk=1024), mm_bf=dict(tm=2048, tn=1024, tk=2048),
                glu=dict(tm=1024, tn=1024, tk=1024), o=dict(tm=512, tn=256, w_pieces=4),
                ffn_out=dict(tm=512, tn=256, tight=True, w_pieces=2),
                ffn=dict(tm=1024, tf=256))


def _run_trunk(x, batch, slen, mem_kv, ssm_state, attn_past, w, wb):
    m = batch * slen
    emit = wb is None
    t = _tiles(m, emit)
    ts = min(slen, 512)
    new_k = new_v = s_re = s_im = None
    copies = {}

    def in_proj(x_in, tensor, li, i):
        tiles = t['mm_bf'] if x_in.dtype == BF16 else t['mm']
        if emit:
            h, copies['in', i] = _matmul(x_in, tensor, li, emit=True, **tiles)
            return h
        return _matmul(x_in, wb['in', i], **tiles)

    def proj_ln(name, a, tensor, x_res, g, b, i):
        if emit:
            xf, xb, copies[name, i] = _matmul_ln(a, tensor, x_res, g, b, i, emit=True, **t[name])
            return xf, xb
        return _matmul_ln(a, wb[name, i], x_res, g, b, i, **t[name])

    x_f32, x_in = x, x
    for i in range(DEPTH):
        li = i // 2
        if i % 2 == 0:
            h = in_proj(x_in, w['w_in_ssm'], li, i)
            if ssm_state is None:
                s0_re = s0_im = jnp.zeros((batch, S5_BLOCKS, 1, S5_HALF), F32)
            else:
                s0_re = ssm_state[0][li].reshape(batch, S5_BLOCKS, 1, S5_HALF)
                s0_im = ssm_state[1][li].reshape(batch, S5_BLOCKS, 1, S5_HALF)
            y, s_re, s_im = _s5(h, w['s5_tables'][li], s0_re, s0_im, batch, slen)
            if emit:
                mix, copies['glu', i] = _glu(y, w['w_glu'], w['b_glu'], li, _act_dtype(ts), emit=True, **t['glu'])
            else:
                mix = _glu(y, wb['glu', i], w['b_glu'], li, _act_dtype(ts), **t['glu'])
            q_col = MIX_WIDTH // MEM_HEAD_DIM
        else:
            h = in_proj(x_in, w['w_in_attn'], li, i)
            lam_init = 0.8 - 0.6 * math.exp(-0.3 * i)
            if attn_past is None:
                mix = _dattn_prompt(h, w, li, lam_init, batch, slen, tq=PROMPT_Q_BLOCK)
            else:
                mix = _dattn_decode(h, attn_past[0], attn_past[1], attn_past[2], w, li, lam_init, batch, slen)
            new_k = _k_rows(h, batch, slen, tm=min(m, 256))
            new_v = _v_heads(h, batch, slen, tm=min(slen, 256))
            q_col = 3 * MIX_WIDTH // MEM_HEAD_DIM
        mem = _mem_attn(h, q_col, mem_kv[i], batch, slen, ts=ts)
        x_f32, x_bf = proj_ln('o', (mix, mem), w['w_o'], x_f32, w['ln1_g'], w['ln1_b'], i)
        if emit:
            hid, copies['ffn_in', i] = _ffn_in(x_bf, w['w_ffn_in'], i, emit=True, **t['ffn'])
        else:
            hid = _ffn_in(x_bf, wb['ffn_in', i], **t['ffn'])
        x_f32, x_in = proj_ln('ffn_out', (hid,), w['w_ffn_out'], x_f32, w['ln2_g'], w['ln2_b'], i)
    return (x_f32, new_k, new_v, s_re, s_im), copies


def kernel(x_prompt, x_sample, cache_attn_k, cache_attn_v, state_ssm_re, state_ssm_im, cache_mem_k, cache_mem_v,
           page_table, mem_prompt, w_in_ssm, ssm_lambda_re, ssm_lambda_im, ssm_log_dt, ssm_b_re, ssm_b_im,
           ssm_c_re, ssm_c_im, ssm_d, w_glu, b_glu, w_in_attn, diff_lambda_q1, diff_lambda_k1, diff_lambda_q2,
           diff_lambda_k2, diff_subln_g, w_mem_kv, w_o, ln1_g, ln1_b, w_ffn_in, w_ffn_out, ln2_g, ln2_b):
    w = dict(w_in_ssm=w_in_ssm, ssm_lambda_re=ssm_lambda_re, ssm_lambda_im=ssm_lambda_im, ssm_log_dt=ssm_log_dt,
             ssm_b_re=ssm_b_re, ssm_b_im=ssm_b_im, ssm_c_re=ssm_c_re, ssm_c_im=ssm_c_im, ssm_d=ssm_d,
             w_glu=w_glu, b_glu=b_glu, w_in_attn=w_in_attn, diff_lambda_q1=diff_lambda_q1,
             diff_lambda_k1=diff_lambda_k1, diff_lambda_q2=diff_lambda_q2, diff_lambda_k2=diff_lambda_k2,
             diff_subln_g=diff_subln_g, w_o=w_o, ln1_g=ln1_g, ln1_b=ln1_b, w_ffn_in=w_ffn_in,
             w_ffn_out=w_ffn_out, ln2_g=ln2_g, ln2_b=ln2_b)
    bp, sp, _ = x_prompt.shape
    bs, ss, _ = x_sample.shape

    mem_rows = mem_prompt.reshape(bp * MEM_TOKENS, D_MODEL)
    mem_kv_p = [_matmul(mem_rows, w_mem_kv, l, tm=bp * MEM_TOKENS, tn=1024, tk=1024) for l in range(DEPTH)]
    mem_k_prompt = jnp.stack([kv[:, :MEM_WIDTH].reshape(bp, MEM_TOKENS, MEM_HEADS, MEM_HEAD_DIM) for kv in mem_kv_p])
    mem_v_prompt = jnp.stack([kv[:, MEM_WIDTH:].reshape(bp, MEM_TOKENS, MEM_HEADS, MEM_HEAD_DIM) for kv in mem_kv_p])
    mem_kv_s = [jnp.concatenate([cache_mem_k[l].reshape(bs * MEM_TOKENS, MEM_WIDTH),
                                 cache_mem_v[l].reshape(bs * MEM_TOKENS, MEM_WIDTH)], axis=1) for l in range(DEPTH)]

    w['s5_tables'] = [_s5_tables(ssm_lambda_re[l], ssm_lambda_im[l], ssm_log_dt[l], ssm_b_re[l], ssm_b_im[l],
                                 ssm_c_re[l], ssm_c_im[l], ssm_d[l]) for l in range(ssm_d.shape[0])]

    (ys, ks, vs, srs, sis), wb = _run_trunk(x_sample.reshape(bs * ss, D_MODEL), bs, ss, mem_kv_s,
                                            (state_ssm_re, state_ssm_im),
                                            (cache_attn_k, cache_attn_v, page_table), w, None)
    (yp, kp, vp, srp, sip), _ = _run_trunk(x_prompt.reshape(bp * sp, D_MODEL), bp, sp, mem_kv_p, None, None, w, wb)

    def sshape(s, b):
        return s.reshape(1, b, SSM_GROUPS, SSM_STATE)

    return (yp.reshape(bp, sp, D_MODEL), ys.reshape(bs, ss, D_MODEL), kp, vp, ks, vs,
            sshape(srp, bp), sshape(sip, bp), sshape(srs, bs), sshape(sis, bs),
            mem_k_prompt, mem_v_prompt)
```
